```python
import math, functools
import jax, jax.numpy as jnp
from jax import lax
import numpy as np

D_MODEL = 1024
BATCH = 16
SEQ = 4096
DEPTH = 1
DEC_BATCH = 128
DEC_SEQ = 8
PAST_LEN = 8192
PAGE_SIZE = 128

D_ATTN = D_MODEL // 2
HEAD_DIM = 64
N_HEADS = D_ATTN // HEAD_DIM
MOBA_BLOCK = 256
MOBA_TOPK = 3
D_POOL = D_MODEL // 2
POOL_WINDOWS = (2, 4, 8, 16)
N_POOL_GROUPS = len(POOL_WINDOWS)
POOL_GROUP = D_POOL // N_POOL_GROUPS
POOL_STATE = max(POOL_WINDOWS) - 1
QUERY_CHUNK = 128
RMS_EPS = 1e-6
SPLIT_SIZES = (D_ATTN, D_ATTN, D_ATTN, D_ATTN, D_POOL, D_POOL, D_MODEL, D_MODEL)
D_IN = sum(SPLIT_SIZES)

kernel_name = 'moba_pool_gated_hybrid_step'


def rmsnorm(x, g):
    xf = x.astype(jnp.float32)
    y = xf * lax.rsqrt(jnp.mean(xf * xf, axis=-1, keepdims=True) + RMS_EPS)
    return (y * g.astype(jnp.float32)).astype(x.dtype)


def alibi_slopes(n):
    start = 2.0 ** (-8.0 / n)
    return jnp.asarray(np.array([start ** (i + 1) for i in range(n)], dtype=np.float32))


def moba_attend(q, k, v, qpos):
    L = k.shape[0]
    nb = -(-L // MOBA_BLOCK)
    pad = nb * MOBA_BLOCK - L
    kp = jnp.pad(k, ((0, pad), (0, 0), (0, 0)))
    vp = jnp.pad(v, ((0, pad), (0, 0), (0, 0)))
    kb = kp.reshape(nb, MOBA_BLOCK, N_HEADS, HEAD_DIM).transpose(2, 0, 1, 3)
    vb = vp.reshape(nb, MOBA_BLOCK, N_HEADS, HEAD_DIM).transpose(2, 0, 1, 3)
    kmean = jnp.mean(kb.astype(jnp.float32), axis=2)
    ksel = min(MOBA_TOPK, nb)
    Q = q.shape[0]
    qc = math.gcd(Q, QUERY_CHUNK)
    nqc = Q // qc
    slopes = alibi_slopes(N_HEADS)
    heads = jnp.arange(N_HEADS)
    offs = jnp.arange(MOBA_BLOCK)
    scale = HEAD_DIM ** -0.5

    def chunk(args):
        qch, pch = args
        own = pch // MOBA_BLOCK
        gate = jnp.einsum('qhd,hnd->qhn', qch.astype(jnp.float32), kmean)
        past = jnp.arange(nb)[None, None, :] < own[:, None, None]
        gate = jnp.where(past, gate, -jnp.inf)
        gval, gidx = lax.top_k(gate, ksel)
        ok = jnp.concatenate([jnp.isfinite(gval), jnp.ones((qc, N_HEADS, 1), bool)], axis=-1)
        idx = jnp.concatenate([gidx, jnp.broadcast_to(own[:, None, None], (qc, N_HEADS, 1))], axis=-1)
        kg = kb[heads[None, :, None], idx]
        vg = vb[heads[None, :, None], idx]
        kpos = idx[..., None] * MOBA_BLOCK + offs
        s = jnp.einsum('qhd,qhnkd->qhnk', qch, kg, preferred_element_type=jnp.float32) * scale
        dist = (pch[:, None, None, None] - kpos).astype(jnp.float32)
        s = s - slopes[None, :, None, None] * dist
        mask = ok[..., None] & (kpos <= pch[:, None, None, None])
        s = jnp.where(mask, s, -jnp.inf)
        p = jax.nn.softmax(s.reshape(qc, N_HEADS, -1), axis=-1)
        o = jnp.einsum('qhm,qhmd->qhd', p.astype(v.dtype), vg.reshape(qc, N_HEADS, -1, HEAD_DIM))
        return o

    out = lax.map(chunk, (q.reshape(nqc, qc, N_HEADS, HEAD_DIM), qpos.reshape(nqc, qc)))
    return out.reshape(Q, N_HEADS, HEAD_DIM)


def prompt_attend(q, k, v):
    qpos = jnp.arange(q.shape[1], dtype=jnp.int32)
    return lax.map(lambda a: moba_attend(a[0], a[1], a[2], qpos), (q, k, v))


def sample_attend(q, k, v, k_pool, v_pool, page_table):
    past_len = page_table.shape[1] * k_pool.shape[1]
    qpos = past_len + jnp.arange(q.shape[1], dtype=jnp.int32)

    def one(a):
        qb, kn, vn, pt = a
        kpast = k_pool[pt].reshape(past_len, N_HEADS, HEAD_DIM)
        vpast = v_pool[pt].reshape(past_len, N_HEADS, HEAD_DIM)
        kall = jnp.concatenate([kpast, kn.astype(kpast.dtype)], axis=0)
        vall = jnp.concatenate([vpast, vn.astype(vpast.dtype)], axis=0)
        return moba_attend(qb, kall, vall, qpos)

    return lax.map(one, (q, k, v, page_table))


def pool_mix(u_ext, t0, w_pool, pool_scale):
    B = u_ext.shape[0]
    T = u_ext.shape[1] - POOL_STATE
    P = POOL_STATE
    uf = u_ext.astype(jnp.float32)
    c = jnp.concatenate([jnp.zeros((B, 1, D_POOL), jnp.float32), jnp.cumsum(uf, axis=1)], axis=1)
    end = c[:, P + 1:]
    pos = t0 + jnp.arange(T)
    outs = []
    for g, w in enumerate(POOL_WINDOWS):
        sl = slice(g * POOL_GROUP, (g + 1) * POOL_GROUP)
        start = c[:, P + 1 - w:P + 1 - w + T, sl]
        cnt = jnp.minimum(pos + 1, w).astype(jnp.float32)[None, :, None]
        outs.append((end[..., sl] - start) / cnt - uf[:, P:, sl])
    pooled = jnp.stack(outs, axis=2)
    mixed = jnp.einsum('btgc,gce->btge', pooled, w_pool.astype(jnp.float32)).reshape(B, T, D_POOL)
    return (mixed * pool_scale.astype(jnp.float32)).astype(u_ext.dtype)


def layer_forward(x, pool_prev, t0, attend, g_pre, w_in, w_pool, pool_scale,
                  w_br_attn, w_br_pool, w_out, g_post):
    B, T, _ = x.shape
    h = rmsnorm(x, g_pre)
    proj = h @ w_in
    split_idx = [int(i) for i in np.cumsum(SPLIT_SIZES)[:-1]]
    q, k, v, z_a, u_b, z_b, gate_a, gate_b = jnp.split(proj, split_idx, axis=-1)
    q = q.reshape(B, T, N_HEADS, HEAD_DIM)
    k = k.reshape(B, T, N_HEADS, HEAD_DIM)
    v = v.reshape(B, T, N_HEADS, HEAD_DIM)
    o_a = attend(q, k, v).reshape(B, T, D_ATTN) * jax.nn.silu(z_a)
    u_ext = jnp.concatenate([pool_prev.astype(u_b.dtype), u_b], axis=1)
    o_b = pool_mix(u_ext, t0, w_pool, pool_scale) * jax.nn.silu(z_b)
    merged = jax.nn.sigmoid(gate_a) * (o_a @ w_br_attn) + jax.nn.sigmoid(gate_b) * (o_b @ w_br_pool)
    y = x + rmsnorm(merged @ w_out, g_post)
    return y, k, v, u_ext[:, -POOL_STATE:]


def setup_inputs(seed: int = 0) -> dict:
    key = jax.random.key(seed)
    ks = jax.random.split(key, 16)
    n_pages = PAST_LEN // PAGE_SIZE
    n_phys = (DEC_BATCH * n_pages * 5) // 4
    f32 = jnp.float32
    x_prompt = jax.random.normal(ks[0], (BATCH, SEQ, D_MODEL), f32)
    x_sample = jax.random.normal(ks[1], (DEC_BATCH, DEC_SEQ, D_MODEL), f32)
    cache_k = jax.random.normal(ks[2], (DEPTH, n_phys, PAGE_SIZE, N_HEADS, HEAD_DIM), f32)
    cache_v = jax.random.normal(ks[3], (DEPTH, n_phys, PAGE_SIZE, N_HEADS, HEAD_DIM), f32)
    state_pool = jax.random.normal(ks[4], (DEPTH, DEC_BATCH, POOL_STATE, D_POOL), f32)
    page_table = jax.random.permutation(ks[5], n_phys)[:DEC_BATCH * n_pages].reshape(DEC_BATCH, n_pages).astype(jnp.int32)
    g_pre = 1.0 + 0.1 * jax.random.normal(ks[6], (DEPTH, D_MODEL), f32)
    w_in = jax.random.normal(ks[7], (DEPTH, D_MODEL, D_IN), f32) * D_MODEL ** -0.5
    w_pool = jax.random.normal(ks[8], (DEPTH, N_POOL_GROUPS, POOL_GROUP, POOL_GROUP), f32) * POOL_GROUP ** -0.5
    pool_scale = 1.0 + 0.1 * jax.random.normal(ks[9], (DEPTH, D_POOL), f32)
    w_br_attn = jax.random.normal(ks[10], (DEPTH, D_ATTN, D_MODEL), f32) * D_ATTN ** -0.5
    w_br_pool = jax.random.normal(ks[11], (DEPTH, D_POOL, D_MODEL), f32) * D_POOL ** -0.5
    w_out = jax.random.normal(ks[12], (DEPTH, D_MODEL, D_MODEL), f32) * D_MODEL ** -0.5
    g_post = 1.0 + 0.1 * jax.random.normal(ks[13], (DEPTH, D_MODEL), f32)
    return {'x_prompt': x_prompt, 'x_sample': x_sample, 'cache_k': cache_k, 'cache_v': cache_v,
            'state_pool': state_pool, 'page_table': page_table, 'g_pre': g_pre, 'w_in': w_in,
            'w_pool': w_pool, 'pool_scale': pool_scale, 'w_br_attn': w_br_attn,
            'w_br_pool': w_br_pool, 'w_out': w_out, 'g_post': g_post}


def reference(x_prompt, x_sample, cache_k, cache_v, state_pool, page_table, g_pre, w_in,
              w_pool, pool_scale, w_br_attn, w_br_pool, w_out, g_post):
    past_len = page_table.shape[1] * cache_k.shape[2]
    yp, ys = x_prompt, x_sample
    kp_l, vp_l, sp_l, ks_l, vs_l, ss_l = [], [], [], [], [], []
    for l in range(DEPTH):
        w = (g_pre[l], w_in[l], w_pool[l], pool_scale[l], w_br_attn[l], w_br_pool[l], w_out[l], g_post[l])
        prev0 = jnp.zeros((yp.shape[0], POOL_STATE, D_POOL), yp.dtype)
        yp, kp, vp, sp = layer_forward(yp, prev0, 0, prompt_attend, *w)
        s_att = functools.partial(sample_attend, k_pool=cache_k[l], v_pool=cache_v[l], page_table=page_table)
        ys, kn, vn, sn = layer_forward(ys, state_pool[l], past_len, s_att, *w)
        kp_l.append(kp); vp_l.append(vp); sp_l.append(sp)
        ks_l.append(kn); vs_l.append(vn); ss_l.append(sn)
    return (yp, ys, jnp.stack(kp_l), jnp.stack(vp_l), jnp.stack(sp_l),
            jnp.stack(ks_l), jnp.stack(vs_l), jnp.stack(ss_l))
```

```python
import functools

import jax
import jax.numpy as jnp
from jax import lax
from jax.experimental import pallas as pl
from jax.experimental.pallas import tpu as pltpu

F32 = jnp.float32
BF16 = jnp.bfloat16

N_HEADS = 8
HEAD_DIM = 64
D_ATTN = N_HEADS * HEAD_DIM
MOBA_BLOCK = 256
MOBA_TOPK = 3
POOL_WINDOWS = (2, 4, 8, 16)
POOL_GROUP = 128
D_POOL = POOL_GROUP * len(POOL_WINDOWS)
POOL_STATE = max(POOL_WINDOWS) - 1
PREV_ROWS = 16
RMS_EPS = 1e-6
LANES = 128
SUBLANES = 8
NEG = -1e30
ALIBI_SLOPES = tuple(2.0 ** (-(h + 1)) for h in range(N_HEADS))

AUX_BIAS = HEAD_DIM
MAX_BLOCKS = 16
AUX_ONE = AUX_BIAS + MAX_BLOCKS
AUX_SUM = HEAD_DIM
D_AUG = N_HEADS * LANES

VMEM_LIMIT = 56 * 1024 * 1024
PAGES_PER_STEP = 8


def _nt_dot(a, b):
    return lax.dot_general(a, b, (((1,), (1,)), ((), ())), preferred_element_type=F32)


def _sigmoid(z):
    return 1.0 / (1.0 + jnp.exp(-z))


def _params(*semantics):
    return pltpu.CompilerParams(dimension_semantics=semantics, vmem_limit_bytes=VMEM_LIMIT)


def _proj_kernel(x_ref, g_ref, wq_ref, wk_ref, wv_ref, wza_ref, wu_ref, wzb_ref, wga_ref, wgb_ref,
                 *outs, aug, tm, blocks_per_seq):
    x = x_ref[...]
    ms = jnp.mean(x * x, axis=-1, keepdims=True)
    h = (x * lax.rsqrt(ms + RMS_EPS) * g_ref[...]).astype(BF16)

    def proj(w_ref):
        return jnp.dot(h, w_ref[...], preferred_element_type=F32)

    q = proj(wq_ref)
    k = proj(wk_ref)
    v = proj(wv_ref)
    za = proj(wza_ref)
    zb = proj(wzb_ref)
    if aug:
        k_ref, v_ref, u_ref, qa_ref, ka_ref, va_ref, sa_ref, sb_ref, ga_ref, gb_ref, km_ref = outs
    else:
        k_ref, v_ref, u_ref, q_ref, sa_ref, sb_ref, ga_ref, gb_ref = outs
        q_ref[...] = q
    k_ref[...] = k
    v_ref[...] = v
    u_ref[...] = proj(wu_ref)
    sa_ref[...] = (za * _sigmoid(za)).astype(sa_ref.dtype)
    sb_ref[...] = (zb * _sigmoid(zb)).astype(sb_ref.dtype)
    ga_ref[...] = _sigmoid(proj(wga_ref)).astype(ga_ref.dtype)
    gb_ref[...] = _sigmoid(proj(wgb_ref)).astype(gb_ref.dtype)
    if not aug:
        return

    nblk = tm // MOBA_BLOCK
    lane = lax.broadcasted_iota(jnp.int32, (MOBA_BLOCK, LANES), 1)
    kl = lax.broadcasted_iota(jnp.int32, (MOBA_BLOCK, LANES), 0).astype(F32)
    is_data = lane < HEAD_DIM
    q_aux = jnp.where(lane == AUX_ONE, 1.0, 0.0)
    v_aux = jnp.where(lane == AUX_SUM, 1.0, 0.0)
    for bl in range(nblk):
        rows = slice(bl * MOBA_BLOCK, (bl + 1) * MOBA_BLOCK)
        blk_in_seq = (pl.program_id(0) * nblk + bl) % blocks_per_seq
        onehot = jnp.where(lane - AUX_BIAS == blk_in_seq, 1.0, 0.0)
        for hh in range(N_HEADS):
            src = slice((hh // 2) * LANES, (hh // 2 + 1) * LANES)
            dst = slice(hh * LANES, (hh + 1) * LANES)

            def head(arr):
                pair = arr[rows, src]
                return pair if hh % 2 == 0 else pltpu.roll(pair, HEAD_DIM, 1)

            qd, kd, vd = head(q), head(k), head(v)
            k_aux = jnp.where(lane == AUX_ONE, ALIBI_SLOPES[hh] * kl, onehot)
            qa_ref[rows, dst] = jnp.where(is_data, qd, q_aux).astype(BF16)
            ka_ref[rows, dst] = jnp.where(is_data, kd, k_aux).astype(BF16)
            va_ref[rows, dst] = jnp.where(is_data, vd, v_aux).astype(BF16)
            km_ref[bl, :, dst] = jnp.mean(kd, axis=0, keepdims=True)


def _project(x2, g, ws, *, aug, seq_len):
    rows, d_model = x2.shape
    tm = MOBA_BLOCK
    assert rows % tm == 0
    inter = BF16 if aug else F32
    row_spec = lambda width: pl.BlockSpec((tm, width), lambda i: (i, 0))
    full = lambda a: pl.BlockSpec(a.shape, lambda i: (0,) * a.ndim)
    shapes = [jax.ShapeDtypeStruct((rows, D_ATTN), F32),
              jax.ShapeDtypeStruct((rows, D_ATTN), F32),
              jax.ShapeDtypeStruct((rows, D_POOL), F32)]
    specs = [row_spec(D_ATTN), row_spec(D_ATTN), row_spec(D_POOL)]
    if aug:
        assert seq_len % MOBA_BLOCK == 0 and seq_len // MOBA_BLOCK <= MAX_BLOCKS
        shapes += [jax.ShapeDtypeStruct((rows, D_AUG), BF16)] * 3
        specs += [row_spec(D_AUG)] * 3
    else:
        shapes += [jax.ShapeDtypeStruct((rows, D_ATTN), F32)]
        specs += [row_spec(D_ATTN)]
    shapes += [jax.ShapeDtypeStruct((rows, D_ATTN), inter), jax.ShapeDtypeStruct((rows, D_POOL), inter),
               jax.ShapeDtypeStruct((rows, d_model), inter), jax.ShapeDtypeStruct((rows, d_model), inter)]
    specs += [row_spec(D_ATTN), row_spec(D_POOL), row_spec(d_model), row_spec(d_model)]
    if aug:
        nblk = tm // MOBA_BLOCK
        shapes += [jax.ShapeDtypeStruct((rows // MOBA_BLOCK, 1, D_AUG), F32)]
        specs += [pl.BlockSpec((nblk, 1, D_AUG), lambda i: (i, 0, 0))]
    kern = functools.partial(_proj_kernel, aug=aug, tm=tm,
                             blocks_per_seq=(seq_len // MOBA_BLOCK) if aug else 1)
    return pl.pallas_call(
        kern, out_shape=shapes, grid=(rows // tm,),
        in_specs=[row_spec(d_model), full(g)] + [full(w) for w in ws],
        out_specs=specs, compiler_params=_params("parallel"),
        name="proj_aug" if aug else "proj_plain",
    )(x2, g, *ws)


def _select_bias(g, slope, i, nb):
    jidx = lax.broadcasted_iota(jnp.int32, g.shape, 0)
    own = jnp.zeros(g.shape, jnp.int32) + i
    rank = jnp.zeros(g.shape, F32)
    for kk in range(nb):
        gk = g[kk:kk + 1, :]
        beats = (gk > g) | ((gk == g) & (kk < jidx))
        rank = rank + jnp.where(beats & (own > kk), 1.0, 0.0)
    sel = (rank < MOBA_TOPK) & (jidx < own)
    rel = ((jidx - own) * MOBA_BLOCK).astype(F32) * slope
    return jnp.where(jidx == own, 0.0, jnp.where(sel, rel, NEG))


def _moba_kernel(q_ref, k_ref, v_ref, km_ref, al_ref, o_ref, qs_ref, m_ref, acc_ref, *, nb):
    tq = MOBA_BLOCK
    lane2 = lax.broadcasted_iota(jnp.int32, (nb, 2 * LANES), 1)
    km = km_ref[...]
    kmp = jnp.concatenate([jnp.where(lane2 < HEAD_DIM, km, 0.0),
                           jnp.where((lane2 >= LANES) & (lane2 < LANES + HEAD_DIM), km, 0.0)],
                          axis=0).astype(BF16)
    row = lax.broadcasted_iota(jnp.int32, (tq, tq), 0)
    col = lax.broadcasted_iota(jnp.int32, (tq, tq), 1)
    causal = col <= row
    lane = lax.broadcasted_iota(jnp.int32, (tq, LANES), 1)

    def attend(e, j, first):
        r0 = pl.multiple_of(j * MOBA_BLOCK, MOBA_BLOCK)
        hl = slice(e * LANES, (e + 1) * LANES)
        s = _nt_dot(qs_ref[e], k_ref[pl.ds(r0, MOBA_BLOCK), hl])
        if first:
            s = jnp.where(causal, s, NEG)
        mx = jnp.max(s, axis=1, keepdims=True)
        if first:
            m_new = jnp.broadcast_to(mx, (tq, LANES))
        else:
            m_prev = m_ref[e]
            m_new = jnp.maximum(m_prev, mx)
        p = jnp.exp(s - jnp.concatenate([m_new, m_new], axis=1))
        pv = jnp.dot(p.astype(BF16), v_ref[pl.ds(r0, MOBA_BLOCK), hl], preferred_element_type=F32)
        if first:
            acc_ref[e] = pv
        else:
            acc_ref[e] = jnp.exp(m_prev - m_new) * acc_ref[e] + pv
        m_ref[e] = m_new

    def qblock(i, carry):
        q0 = pl.multiple_of(i * tq, tq)
        q2 = q_ref[pl.ds(q0, tq), :]
        g_t = _nt_dot(kmp, q2)
        for e in range(2):
            slope = jnp.concatenate([al_ref[e:e + 1, :]] * (tq // LANES), axis=1)
            bias_t = _select_bias(g_t[e * nb:(e + 1) * nb, :], slope, i, nb)
            padded = jnp.concatenate([jnp.zeros((AUX_BIAS, tq), F32), bias_t,
                                      jnp.zeros((LANES - AUX_BIAS - nb, tq), F32)], axis=0)
            qs_ref[e] = q2[:, e * LANES:(e + 1) * LANES] + padded.T.astype(BF16)
        for e in range(2):
            attend(e, i, True)

        def kvblock(j, c):
            for e in range(2):
                attend(e, j, False)
            return c

        lax.fori_loop(0, i, kvblock, 0)
        outs = []
        for e in range(2):
            acc = acc_ref[e]
            outs.append(acc / acc[:, AUX_SUM:AUX_SUM + 1])
        o_pair = jnp.where(lane < HEAD_DIM, outs[0], pltpu.roll(outs[1], HEAD_DIM, 1))
        o_ref[pl.ds(q0, tq), :] = o_pair.astype(o_ref.dtype)
        return carry

    lax.fori_loop(0, nb, qblock, 0)


def _moba_prompt(qa, ka, va, km, batch, seq_len):
    nb = seq_len // MOBA_BLOCK
    pair = 2 * LANES
    slopes = jnp.broadcast_to(jnp.asarray(ALIBI_SLOPES, F32).reshape(N_HEADS // 2, 2, 1),
                              (N_HEADS // 2, 2, LANES))
    seq_spec = lambda width: pl.BlockSpec((None, seq_len, width), lambda b, p: (b, 0, p))
    return pl.pallas_call(
        functools.partial(_moba_kernel, nb=nb),
        out_shape=jax.ShapeDtypeStruct((batch, seq_len, D_ATTN), BF16),
        grid=(batch, N_HEADS // 2),
        in_specs=[seq_spec(pair), seq_spec(pair), seq_spec(pair),
                  pl.BlockSpec((None, nb, pair), lambda b, p: (b, 0, p)),
                  pl.BlockSpec((None, 2, LANES), lambda b, p: (p, 0, 0))],
        out_specs=seq_spec(LANES),
        scratch_shapes=[pltpu.VMEM((2, MOBA_BLOCK, LANES), BF16),
                        pltpu.VMEM((2, MOBA_BLOCK, LANES), F32),
                        pltpu.VMEM((2, MOBA_BLOCK, LANES), F32)],
        compiler_params=_params("parallel", "parallel"),
        name="moba_prompt",
    )(qa.reshape(batch, seq_len, D_AUG), ka.reshape(batch, seq_len, D_AUG),
      va.reshape(batch, seq_len, D_AUG), km.reshape(batch, nb, D_AUG), slopes)


def _head_rows(t):
    r = lax.broadcasted_iota(jnp.int32, (N_HEADS * t, 1), 0)
    return r // t, r % t


def _row_slopes(t):
    head, _ = _head_rows(t)
    slope = jnp.zeros((N_HEADS * t, 1), F32)
    for hh in range(N_HEADS):
        slope = jnp.where(head == hh, ALIBI_SLOPES[hh], slope)
    return slope


def _skeys_kernel(pt_ref, q_ref, kn_ref, *rest, t, nblocks, past_len):
    del pt_ref
    pages = rest[:PAGES_PER_STEP]
    pp_ref, po_ref, qbd_ref, s_ref, km_ref = rest[PAGES_PER_STEP:]
    rows = N_HEADS * t
    c = pl.program_id(1)
    blocks_per_step = PAGES_PER_STEP // 2

    @pl.when(c == 0)
    def _():
        qt = jnp.concatenate([q_ref[...]] * N_HEADS, axis=0)
        head = lax.broadcasted_iota(jnp.int32, (rows, D_ATTN), 0) // t
        lane_head = lax.broadcasted_iota(jnp.int32, (rows, D_ATTN), 1) // HEAD_DIM
        qbd_ref[...] = jnp.where(head == lane_head, qt, 0.0).astype(BF16)
        km_ref[...] = jnp.zeros(km_ref.shape, F32)

    qbd = qbd_ref[...]
    km_lane = lax.broadcasted_iota(jnp.int32, (D_ATTN, LANES), 1)
    km = km_ref[...]
    for bl in range(blocks_per_step):
        kt = jnp.concatenate([pages[2 * bl][...], pages[2 * bl + 1][...]], axis=1)
        j = c * blocks_per_step + bl
        km = jnp.where(km_lane == j, jnp.sum(kt, axis=1, keepdims=True) * (1.0 / MOBA_BLOCK), km)
        s_ref[j] = jnp.dot(qbd, kt.astype(BF16), preferred_element_type=F32)
    km_ref[...] = km

    @pl.when(c == pl.num_programs(1) - 1)
    def _():
        g = jnp.dot(qbd, km.astype(BF16), preferred_element_type=F32)
        jidx = lax.broadcasted_iota(jnp.int32, g.shape, 1)
        rank = jnp.zeros(g.shape, F32)
        for kk in range(nblocks):
            gk = g[:, kk:kk + 1]
            rank = rank + jnp.where((gk > g) | ((gk == g) & (kk < jidx)), 1.0, 0.0)
        unsel = jnp.where(rank < MOBA_TOPK, 0.0, NEG)
        slope = _row_slopes(t)
        _, qq = _head_rows(t)
        kl = lax.broadcasted_iota(jnp.int32, (rows, MOBA_BLOCK), 1)
        kn = jnp.concatenate([kn_ref[...], jnp.zeros((LANES - t, D_ATTN), F32)], axis=0).astype(BF16)
        r_own = lax.broadcasted_iota(jnp.int32, (rows, LANES), 1)
        s_own = _nt_dot(qbd, kn) + slope * r_own.astype(F32)
        s_own = jnp.where(r_own <= qq, s_own, NEG)
        m = jnp.max(s_own, axis=1, keepdims=True)
        for j in range(nblocks):
            rel = (kl + (j * MOBA_BLOCK - past_len)).astype(F32)
            sj = s_ref[j] + slope * rel + unsel[:, j:j + 1]
            s_ref[j] = sj
            m = jnp.maximum(m, jnp.max(sj, axis=1, keepdims=True))
        p_own = jnp.exp(s_own - m)
        l = jnp.sum(p_own, axis=1, keepdims=True)
        for j in range(nblocks):
            pj = jnp.exp(s_ref[j] - m)
            s_ref[j] = pj
            l = l + jnp.sum(pj, axis=1, keepdims=True)
        inv = 1.0 / l
        po_ref[...] = (p_own * inv).astype(po_ref.dtype)
        for j in range(nblocks):
            pp_ref[j] = (s_ref[j] * inv).astype(pp_ref.dtype)


def _svals_kernel(pt_ref, pp_ref, po_ref, vn_ref, *rest, t):
    del pt_ref
    pages = rest[:PAGES_PER_STEP]
    o_ref, acc_ref = rest[PAGES_PER_STEP:]
    rows = N_HEADS * t
    c = pl.program_id(1)

    @pl.when(c == 0)
    def _():
        vn = jnp.concatenate([vn_ref[...], jnp.zeros((LANES - t, D_ATTN), F32)], axis=0).astype(BF16)
        acc_ref[...] = jnp.dot(po_ref[...], vn, preferred_element_type=F32)

    acc = acc_ref[...]
    for bl in range(PAGES_PER_STEP // 2):
        vt = jnp.concatenate([pages[2 * bl][...], pages[2 * bl + 1][...]], axis=1).astype(BF16)
        acc = acc + _nt_dot(pp_ref[bl], vt)
    acc_ref[...] = acc

    @pl.when(c == pl.num_programs(1) - 1)
    def _():
        lane_head = lax.broadcasted_iota(jnp.int32, (t, D_ATTN), 1) // HEAD_DIM
        out = jnp.zeros((t, D_ATTN), F32)
        for hh in range(N_HEADS):
            out = jnp.where(lane_head == hh, acc[hh * t:(hh + 1) * t, :], out)
        o_ref[...] = out


def _moba_sample(q, kn, vn, cache_k, cache_v, page_table, t):
    db, n_pages = page_table.shape
    page = cache_k.shape[2]
    past_len = n_pages * page
    assert page == LANES and 2 * page == MOBA_BLOCK and n_pages % PAGES_PER_STEP == 0
    assert t == SUBLANES and past_len % MOBA_BLOCK == 0
    nblocks = past_len // MOBA_BLOCK
    assert nblocks <= LANES
    steps = n_pages // PAGES_PER_STEP
    bps = PAGES_PER_STEP // 2
    rows = N_HEADS * t
    pt = page_table.reshape(-1).astype(jnp.int32)

    def page_spec(i):
        return pl.BlockSpec((None, D_ATTN, page),
                            lambda b, c, pt_ref: (pt_ref[b * n_pages + c * PAGES_PER_STEP + i], 0, 0))

    new_spec = pl.BlockSpec((t, D_ATTN), lambda b, c, pt_ref: (b, 0))
    pp, po = pl.pallas_call(
        functools.partial(_skeys_kernel, t=t, nblocks=nblocks, past_len=past_len),
        out_shape=[jax.ShapeDtypeStruct((db, nblocks, rows, MOBA_BLOCK), BF16),
                   jax.ShapeDtypeStruct((db, rows, LANES), BF16)],
        grid_spec=pltpu.PrefetchScalarGridSpec(
            num_scalar_prefetch=1, grid=(db, steps),
            in_specs=[new_spec, new_spec] + [page_spec(i) for i in range(PAGES_PER_STEP)],
            out_specs=[pl.BlockSpec((None, nblocks, rows, MOBA_BLOCK), lambda b, c, pt_ref: (b, 0, 0, 0)),
                       pl.BlockSpec((None, rows, LANES), lambda b, c, pt_ref: (b, 0, 0))],
            scratch_shapes=[pltpu.VMEM((rows, D_ATTN), BF16),
                            pltpu.VMEM((nblocks, rows, MOBA_BLOCK), F32),
                            pltpu.VMEM((D_ATTN, LANES), F32)]),
        compiler_params=_params("parallel", "arbitrary"),
        name="sample_keys",
    )(pt, q, kn, *([cache_k] * PAGES_PER_STEP))
    return pl.pallas_call(
        functools.partial(_svals_kernel, t=t),
        out_shape=jax.ShapeDtypeStruct((db * t, D_ATTN), F32),
        grid_spec=pltpu.PrefetchScalarGridSpec(
            num_scalar_prefetch=1, grid=(db, steps),
            in_specs=[pl.BlockSpec((None, bps, rows, MOBA_BLOCK), lambda b, c, pt_ref: (b, c, 0, 0)),
                      pl.BlockSpec((None, rows, LANES), lambda b, c, pt_ref: (b, 0, 0)),
                      new_spec] + [page_spec(i) for i in range(PAGES_PER_STEP)],
            out_specs=new_spec,
            scratch_shapes=[pltpu.VMEM((rows, D_ATTN), F32)]),
        compiler_params=_params("parallel", "arbitrary"),
        name="sample_values",
    )(pt, pp, po, vn, *([cache_v] * PAGES_PER_STEP))


def _merge_kernel(o_ref, sa_ref, u_ref, up_ref, sb_ref, ga_ref, gb_ref, x_ref,
                  wp_ref, ps_ref, wba_ref, wbp_ref, wo_ref, gp_ref, y_ref, ubuf_ref,
                  *, ns, tm, t0, zero_first_prev):
    i = pl.program_id(1)
    n = ns * tm
    prev = up_ref[...]
    if zero_first_prev:
        prev = jnp.where(i == 0, 0.0, prev)
    ubuf_ref[:, 0:PREV_ROWS, :] = prev
    ubuf_ref[:, PREV_ROWS:PREV_ROWS + tm, :] = u_ref[...]
    pos = t0 + i * tm + lax.broadcasted_iota(jnp.int32, (ns, tm, POOL_GROUP), 1)
    mixed = []
    for gidx, w in enumerate(POOL_WINDOWS):
        gl = slice(gidx * POOL_GROUP, (gidx + 1) * POOL_GROUP)
        cur = ubuf_ref[:, PREV_ROWS:PREV_ROWS + tm, gl]
        tot = cur
        for s in range(1, w):
            tot = tot + ubuf_ref[:, PREV_ROWS - s:PREV_ROWS - s + tm, gl]
        cnt = jnp.minimum(pos + 1, w).astype(F32)
        pooled = (tot / cnt - cur).reshape(n, POOL_GROUP)
        mixed.append(jnp.dot(pooled.astype(BF16), wp_ref[gidx], preferred_element_type=F32))
    mixed = jnp.concatenate(mixed, axis=1) * ps_ref[...]

    def flat(ref):
        return ref[...].reshape(n, ref.shape[-1]).astype(F32)

    o_b = (mixed * flat(sb_ref)).astype(BF16)
    o_a = (flat(o_ref) * flat(sa_ref)).astype(BF16)
    m_a = jnp.dot(o_a, wba_ref[...], preferred_element_type=F32)
    m_b = jnp.dot(o_b, wbp_ref[...], preferred_element_type=F32)
    merged = flat(ga_ref) * m_a + flat(gb_ref) * m_b
    z = jnp.dot(merged.astype(BF16), wo_ref[...], preferred_element_type=F32)
    ms = jnp.mean(z * z, axis=-1, keepdims=True)
    y = flat(x_ref) + z * lax.rsqrt(ms + RMS_EPS) * gp_ref[...]
    y_ref[...] = y.reshape(y_ref.shape)


def _merge(o, sa, u, u_prev, sb, ga, gb, x, ws, *, ns, tm, t0, zero_first_prev):
    nseq, seq_len, d_model = x.shape
    assert nseq % ns == 0 and seq_len % tm == 0 and tm % SUBLANES == 0
    assert ns == 1 or (tm == SUBLANES and all(a.dtype == F32 for a in (o, sa, sb, ga, gb)))
    tile = lambda width: pl.BlockSpec((ns, tm, width), lambda b, i: (b, i, 0))
    full = lambda a: pl.BlockSpec(a.shape, lambda b, i: (0,) * a.ndim)
    per = tm // PREV_ROWS if tm >= PREV_ROWS else 0
    prev_spec = pl.BlockSpec((ns, PREV_ROWS, D_POOL),
                             lambda b, i: (b, jnp.maximum(i * per - 1, 0), 0))
    kern = functools.partial(_merge_kernel, ns=ns, tm=tm, t0=t0, zero_first_prev=zero_first_prev)
    return pl.pallas_call(
        kern, out_shape=jax.ShapeDtypeStruct(x.shape, F32),
        grid=(nseq // ns, seq_len // tm),
        in_specs=[tile(D_ATTN), tile(D_ATTN), tile(D_POOL), prev_spec, tile(D_POOL),
                  tile(d_model), tile(d_model), tile(d_model)] + [full(w) for w in ws],
        out_specs=tile(d_model),
        scratch_shapes=[pltpu.VMEM((ns, PREV_ROWS + tm, D_POOL), F32)],
        compiler_params=_params("parallel", "arbitrary"),
        name="merge_prompt" if ns == 1 else "merge_sample",
    )(o, sa, u, u_prev, sb, ga, gb, x, *ws)


def _layer(yp, ys, cache_k, cache_v, state, page_table, g_pre, w_in, w_pool, pool_scale,
           w_br_attn, w_br_pool, w_out, g_post):
    batch, seq_len, d_model = yp.shape
    db, t, _ = ys.shape
    past_len = page_table.shape[1] * cache_k.shape[2]

    bounds = [0]
    for width in (D_ATTN, D_ATTN, D_ATTN, D_ATTN, D_POOL, D_POOL, d_model, d_model):
        bounds.append(bounds[-1] + width)
    wq, wk, wv, wza, wu, wzb, wga, wgb = [w_in[:, a:b] for a, b in zip(bounds[:-1], bounds[1:])]
    proj_ws = [w.astype(BF16) for w in (wq * HEAD_DIM ** -0.5, wk, wv, wza, wu, wzb, wga, wgb)]
    g_pre2 = g_pre.reshape(1, d_model)
    merge_ws = [w_pool.astype(BF16), pool_scale.reshape(1, D_POOL), w_br_attn.astype(BF16),
                w_br_pool.astype(BF16), w_out.astype(BF16), g_post.reshape(1, d_model)]

    kp, vp, up, qa, ka, va, sa, sb, ga, gb, km = _project(
        yp.reshape(batch * seq_len, d_model), g_pre2, proj_ws, aug=True, seq_len=seq_len)
    op = _moba_prompt(qa, ka, va, km, batch, seq_len)
    r3 = lambda a: a.reshape(batch, seq_len, a.shape[-1])
    up3 = r3(up)
    y_prompt = _merge(op, r3(sa), up3, up3, r3(sb), r3(ga), r3(gb), yp, merge_ws,
                      ns=1, tm=512, t0=0, zero_first_prev=True)

    ks, vs, us, qs, sas, sbs, gas, gbs = _project(
        ys.reshape(db * t, d_model), g_pre2, proj_ws, aug=False, seq_len=t)
    osamp = _moba_sample(qs, ks, vs, cache_k, cache_v, page_table, t)
    s3 = lambda a: a.reshape(db, t, a.shape[-1])
    us3 = s3(us)
    state16 = jnp.concatenate([jnp.zeros((db, PREV_ROWS - POOL_STATE, D_POOL), F32), state], axis=1)
    y_sample = _merge(s3(osamp), s3(sas), us3, state16, s3(sbs), s3(gas), s3(gbs), ys, merge_ws,
                      ns=16, tm=t, t0=past_len, zero_first_prev=False)

    heads = lambda a, n, s: a.reshape(n, s, N_HEADS, HEAD_DIM)
    pool_prompt = up3[:, seq_len - POOL_STATE:]
    pool_sample = jnp.concatenate([state, us3], axis=1)[:, -POOL_STATE:]
    return (y_prompt, y_sample, heads(kp, batch, seq_len), heads(vp, batch, seq_len), pool_prompt,
            heads(ks, db, t), heads(vs, db, t), pool_sample)


def kernel(x_prompt, x_sample, cache_k, cache_v, state_pool, page_table, g_pre, w_in, w_pool,
           pool_scale, w_br_attn, w_br_pool, w_out, g_post):
    depth = w_in.shape[0]
    n_phys, page = cache_k.shape[1], cache_k.shape[2]
    pages_t = lambda c: jnp.transpose(c, (0, 2, 3, 1)).reshape(n_phys, D_ATTN, page)
    yp, ys = x_prompt, x_sample
    per_layer = []
    for l in range(depth):
        outs = _layer(yp, ys, pages_t(cache_k[l]), pages_t(cache_v[l]),
                      state_pool[l], page_table, g_pre[l], w_in[l], w_pool[l], pool_scale[l],
                      w_br_attn[l], w_br_pool[l], w_out[l], g_post[l])
        yp, ys = outs[0], outs[1]
        per_layer.append(outs[2:])
    stacked = [jnp.stack([layer[i] for layer in per_layer]) for i in range(6)]
    return (yp, ys, *stacked)
```

```python
import functools

import numpy as np
import jax
import jax.numpy as jnp
from jax import lax
from jax.experimental import pallas as pl
from jax.experimental.pallas import tpu as pltpu

F32 = jnp.float32
BF16 = jnp.bfloat16

N_HEADS = 8
HEAD_DIM = 64
D_ATTN = N_HEADS * HEAD_DIM
MOBA_BLOCK = 256
MOBA_TOPK = 3
POOL_WINDOWS = (2, 4, 8, 16)
POOL_GROUP = 128
D_POOL = POOL_GROUP * len(POOL_WINDOWS)
POOL_STATE = max(POOL_WINDOWS) - 1
PREV_ROWS = 16
RMS_EPS = 1e-6
LANES = 128
SUBLANES = 8
BF16_ROWS = 16
NEG = -1e30
ALIBI_SLOPES = tuple(2.0 ** (-(h + 1)) for h in range(N_HEADS))

AUX_BIAS = HEAD_DIM
MAX_BLOCKS = 16
AUX_ONE = AUX_BIAS + MAX_BLOCKS
D_AUG = N_HEADS * LANES
V_ROWS = HEAD_DIM + BF16_ROWS
D_VT = N_HEADS * V_ROWS

MOBA_HEADS = 4
MOBA_LAG = 2
MOBA_RING = 2 * MOBA_LAG
MOBA_UNROLL = MOBA_RING
VMEM_LIMIT = 56 * 1024 * 1024
PAGES_PER_STEP = 16


def _nt_dot(a, b):
    return lax.dot_general(a, b, (((1,), (1,)), ((), ())), preferred_element_type=F32)


def _dot(a, b):
    return jnp.dot(a, b, preferred_element_type=F32)


def _sigmoid(z):
    return 1.0 / (1.0 + jnp.exp(-z))


def _params(*semantics):
    return pltpu.CompilerParams(dimension_semantics=semantics, vmem_limit_bytes=VMEM_LIMIT)


def _proj_kernel(x_ref, g_ref, wq_ref, wk_ref, wv_ref, wza_ref, wu_ref, wzb_ref, wga_ref, wgb_ref,
                 *outs, prompt, blocks_per_seq):
    x = x_ref[...]
    ms = jnp.mean(x * x, axis=-1, keepdims=True)
    h = (x * lax.rsqrt(ms + RMS_EPS) * g_ref[...]).astype(BF16)

    def proj(w_ref):
        return _dot(h, w_ref[...])

    q = proj(wq_ref)
    k = proj(wk_ref)
    v = proj(wv_ref)
    za = proj(wza_ref)
    zb = proj(wzb_ref)
    if prompt:
        kt_ref, vt_ref, u_ref, qt_ref, ka_ref, va_ref, sa_ref, sb_ref, ga_ref, gb_ref, km_ref = outs
    else:
        k_ref, v_ref, u_ref, q_ref, sa_ref, sb_ref, ga_ref, gb_ref = outs
        q_ref[...] = q
        k_ref[...] = k
        v_ref[...] = v
    u_ref[...] = proj(wu_ref)
    sa_ref[...] = (za * _sigmoid(za)).astype(sa_ref.dtype)
    sb_ref[...] = (zb * _sigmoid(zb)).astype(sb_ref.dtype)
    ga_ref[...] = _sigmoid(proj(wga_ref)).astype(ga_ref.dtype)
    gb_ref[...] = _sigmoid(proj(wgb_ref)).astype(gb_ref.dtype)
    if not prompt:
        return

    v_t = v.T
    kt_ref[...] = k.T
    vt_ref[...] = v_t
    qt_ref[...] = q.T.astype(BF16)
    ones_rows = jnp.where(lax.broadcasted_iota(jnp.int32, (BF16_ROWS, MOBA_BLOCK), 0) == 0, 1.0, 0.0)
    pieces = []
    for hh in range(N_HEADS):
        pieces += [v_t[hh * HEAD_DIM:(hh + 1) * HEAD_DIM], ones_rows]
    va_ref[...] = jnp.concatenate(pieces, axis=0).astype(BF16)

    lane = lax.broadcasted_iota(jnp.int32, (MOBA_BLOCK, LANES), 1)
    kl = lax.broadcasted_iota(jnp.int32, (MOBA_BLOCK, LANES), 0).astype(F32)
    blk_in_seq = pl.program_id(0) % blocks_per_seq
    onehot = jnp.where(lane - AUX_BIAS == blk_in_seq, 1.0, 0.0)
    for hh in range(N_HEADS):
        pair = k[:, (hh // 2) * LANES:(hh // 2 + 1) * LANES]
        kd = pair if hh % 2 == 0 else pltpu.roll(pair, HEAD_DIM, 1)
        k_aux = jnp.where(lane == AUX_ONE, ALIBI_SLOPES[hh] * kl, onehot)
        dst = slice(hh * LANES, (hh + 1) * LANES)
        ka_ref[:, dst] = jnp.where(lane < HEAD_DIM, kd, k_aux).astype(BF16)
        km_ref[:, dst] = jnp.mean(kd, axis=0, keepdims=True)


def _project(x2, g, ws, *, prompt, seq_len):
    rows, d_model = x2.shape
    tm = MOBA_BLOCK
    assert rows % tm == 0
    inter = BF16 if prompt else F32
    row_spec = lambda width: pl.BlockSpec((tm, width), lambda i: (i, 0))
    full = lambda a: pl.BlockSpec(a.shape, lambda i: (0,) * a.ndim)
    sds = jax.ShapeDtypeStruct
    if prompt:
        assert seq_len % MOBA_BLOCK == 0 and seq_len // MOBA_BLOCK <= MAX_BLOCKS
        nb = seq_len // MOBA_BLOCK
        batch = rows // seq_len
        t_spec = pl.BlockSpec((None, D_ATTN, tm), lambda i: (i // nb, 0, i % nb))
        blk_spec = lambda r: pl.BlockSpec((None, None, r, tm), lambda i: (i // nb, i % nb, 0, 0))
        shapes = [sds((batch, D_ATTN, seq_len), F32), sds((batch, D_ATTN, seq_len), F32),
                  sds((rows, D_POOL), F32),
                  sds((batch, nb, D_ATTN, tm), BF16),
                  sds((rows, D_AUG), BF16),
                  sds((batch, nb, D_VT, tm), BF16)]
        specs = [t_spec, t_spec, row_spec(D_POOL), blk_spec(D_ATTN), row_spec(D_AUG), blk_spec(D_VT)]
    else:
        shapes = [sds((rows, D_ATTN), F32), sds((rows, D_ATTN), F32), sds((rows, D_POOL), F32),
                  sds((rows, D_ATTN), F32)]
        specs = [row_spec(D_ATTN), row_spec(D_ATTN), row_spec(D_POOL), row_spec(D_ATTN)]
    shapes += [sds((rows, D_ATTN), inter), sds((rows, D_POOL), inter),
               sds((rows, d_model), inter), sds((rows, d_model), inter)]
    specs += [row_spec(D_ATTN), row_spec(D_POOL), row_spec(d_model), row_spec(d_model)]
    if prompt:
        shapes += [sds((rows // tm, 1, D_AUG), F32)]
        specs += [pl.BlockSpec((None, 1, D_AUG), lambda i: (i, 0, 0))]
    kern = functools.partial(_proj_kernel, prompt=prompt,
                             blocks_per_seq=(seq_len // MOBA_BLOCK) if prompt else 1)
    return pl.pallas_call(
        kern, out_shape=shapes, grid=(rows // tm,),
        in_specs=[row_spec(d_model), full(g)] + [full(w) for w in ws],
        out_specs=specs, compiler_params=_params("parallel"),
        name="proj_prompt" if prompt else "proj_sample",
    )(x2, g, *ws)


def _select_bias(g, slope, i, nb):
    jidx = lax.broadcasted_iota(jnp.int32, g.shape, 0)
    own = jnp.zeros(g.shape, jnp.int32) + i
    rank = jnp.zeros(g.shape, F32)
    for kk in range(nb):
        gk = g[kk:kk + 1, :]
        beats = (gk > g) | ((gk == g) & (kk < jidx))
        rank = rank + jnp.where(beats & (own > kk), 1.0, 0.0)
    sel = (rank < MOBA_TOPK) & (jidx < own)
    rel = ((jidx - own) * MOBA_BLOCK).astype(F32) * slope
    return jnp.where(jidx == own, 0.0, jnp.where(sel, rel, NEG))


def _moba_schedule(items, nb):
    n, lag = len(items), MOBA_LAG
    steps = -(-(n + 2 * lag) // MOBA_UNROLL) * MOBA_UNROLL
    tab = np.zeros((5, steps), np.int32)
    for tau in range(steps):
        tab[1, tau], tab[0, tau] = items[tau] if tau < n else (nb - 1, 0)
        tab[2, tau] = items[tau - lag][0] if lag <= tau < n + lag else nb
        tab[4, tau], tab[3, tau] = items[tau - 2 * lag] if 2 * lag <= tau < n + 2 * lag else (nb, 0)
    return tab, steps


def _moba_kernel(own_ref, past_ref, qt_ref, k_ref, vt_ref, km_ref, al_ref, o_ref,
                 qs_ref, s_ref, p_ref, a_ref, m_ref, acc_ref, cb_ref, *, nb, own_steps, past_steps):
    tq = MOBA_BLOCK
    key_pos = lax.broadcasted_iota(jnp.int32, (MOBA_BLOCK, tq), 0)
    qry_pos = lax.broadcasted_iota(jnp.int32, (MOBA_BLOCK, tq), 1)
    cb_ref[...] = jnp.where(key_pos <= qry_pos, 0.0, NEG)
    for ref in (s_ref, p_ref, a_ref):
        ref[...] = jnp.zeros(ref.shape, ref.dtype)
    m_ref[nb] = jnp.zeros(m_ref.shape[1:], F32)
    acc_ref[nb] = jnp.zeros(acc_ref.shape[1:], F32)

    km = km_ref[...].astype(BF16)
    one_row = jnp.where(lax.broadcasted_iota(jnp.int32, (BF16_ROWS, tq), 0) == 0, 1.0, 0.0).astype(BF16)
    pad_rows = jnp.zeros((LANES - AUX_ONE - BF16_ROWS, tq), BF16)

    def gating(i, carry):
        qt = qt_ref[i]
        for e in range(MOBA_HEADS):
            qe = qt[e * HEAD_DIM:(e + 1) * HEAD_DIM]
            gate = _dot(km[:, e * LANES:e * LANES + HEAD_DIM], qe)
            slope = jnp.concatenate([al_ref[e:e + 1, :]] * (tq // LANES), axis=1)
            bias = _select_bias(gate, slope, i, nb)
            if nb < MAX_BLOCKS:
                bias = jnp.concatenate([bias, jnp.zeros((MAX_BLOCKS - nb, tq), F32)], axis=0)
            qs_ref[i, e] = jnp.concatenate([qe, bias.astype(BF16), one_row, pad_rows], axis=0)
        return carry

    lax.fori_loop(0, nb, gating, 0)

    def value_stage(tab_ref, tau, u, own):
        ring = u % MOBA_RING
        vblk = tab_ref[3, tau]
        slot = tab_ref[4, tau]
        for e in range(MOBA_HEADS):
            pv = _dot(vt_ref[vblk, e * V_ROWS:(e + 1) * V_ROWS, :], p_ref[ring, e])
            acc_ref[slot, e] = pv if own else a_ref[ring, e] * acc_ref[slot, e] + pv

    def softmax_stage(tab_ref, tau, u, own):
        ring = (u + MOBA_LAG) % MOBA_RING
        slot = tab_ref[2, tau]
        for e in range(MOBA_HEADS):
            s = s_ref[ring, e]
            m_new = jnp.max(s, axis=0, keepdims=True)
            if not own:
                m_prev = m_ref[slot, e]
                m_new = jnp.maximum(m_prev, m_new)
                a_ref[ring, e] = jnp.exp(m_prev - m_new)
            p_ref[ring, e] = jnp.exp(s - m_new).astype(BF16)
            m_ref[slot, e] = m_new

    def score_stage(tab_ref, tau, u, own):
        ring = u % MOBA_RING
        r0 = pl.multiple_of(tab_ref[0, tau] * MOBA_BLOCK, MOBA_BLOCK)
        qblk = tab_ref[1, tau]
        for e in range(MOBA_HEADS):
            s = _dot(k_ref[pl.ds(r0, MOBA_BLOCK), e * LANES:(e + 1) * LANES], qs_ref[qblk, e])
            s_ref[ring, e] = s + cb_ref[...] if own else s

    def pipeline(tab_ref, steps, own):
        def body(it, carry):
            for u in range(MOBA_UNROLL):
                for stage in (score_stage, value_stage, softmax_stage):
                    stage(tab_ref, MOBA_UNROLL * it + u, u, own)
            return carry

        lax.fori_loop(0, steps // MOBA_UNROLL, body, 0)

    pipeline(own_ref, own_steps, True)
    pipeline(past_ref, past_steps, False)

    def finalize(i, carry):
        outs = []
        for e in range(MOBA_HEADS):
            acc = acc_ref[i, e]
            outs.append(acc[0:HEAD_DIM] / acc[HEAD_DIM:HEAD_DIM + 1])
        q0 = pl.multiple_of(i * tq, tq)
        o_ref[pl.ds(q0, tq), :] = jnp.concatenate(outs, axis=0).T.astype(o_ref.dtype)
        return carry

    lax.fori_loop(0, nb, finalize, 0)


def _moba_prompt(qt, ka, va, km, batch, seq_len):
    nb = seq_len // MOBA_BLOCK
    assert nb <= MAX_BLOCKS and N_HEADS % MOBA_HEADS == 0
    groups = N_HEADS // MOBA_HEADS
    own_tab, own_steps = _moba_schedule([(i, i) for i in range(nb)], nb)
    past_tab, past_steps = _moba_schedule([(i, j) for i in range(nb) for j in range(i)], nb)
    slopes = jnp.broadcast_to(jnp.asarray(ALIBI_SLOPES, F32).reshape(groups, MOBA_HEADS, 1),
                              (groups, MOBA_HEADS, LANES))
    blk_spec = lambda r: pl.BlockSpec((None, nb, r, MOBA_BLOCK), lambda b, p, *_: (b, 0, p, 0))
    seq_spec = lambda w: pl.BlockSpec((None, seq_len, w), lambda b, p, *_: (b, 0, p))
    return pl.pallas_call(
        functools.partial(_moba_kernel, nb=nb, own_steps=own_steps, past_steps=past_steps),
        out_shape=jax.ShapeDtypeStruct((batch, seq_len, D_ATTN), BF16),
        grid_spec=pltpu.PrefetchScalarGridSpec(
            num_scalar_prefetch=2, grid=(batch, groups),
            in_specs=[blk_spec(MOBA_HEADS * HEAD_DIM), seq_spec(MOBA_HEADS * LANES),
                      blk_spec(MOBA_HEADS * V_ROWS),
                      pl.BlockSpec((None, nb, MOBA_HEADS * LANES), lambda b, p, *_: (b, 0, p)),
                      pl.BlockSpec((None, MOBA_HEADS, LANES), lambda b, p, *_: (p, 0, 0))],
            out_specs=seq_spec(MOBA_HEADS * HEAD_DIM),
            scratch_shapes=[pltpu.VMEM((nb, MOBA_HEADS, LANES, MOBA_BLOCK), BF16),
                            pltpu.VMEM((MOBA_RING, MOBA_HEADS, MOBA_BLOCK, MOBA_BLOCK), F32),
                            pltpu.VMEM((MOBA_RING, MOBA_HEADS, MOBA_BLOCK, MOBA_BLOCK), BF16),
                            pltpu.VMEM((MOBA_RING, MOBA_HEADS, 1, MOBA_BLOCK), F32),
                            pltpu.VMEM((nb + 1, MOBA_HEADS, 1, MOBA_BLOCK), F32),
                            pltpu.VMEM((nb + 1, MOBA_HEADS, V_ROWS, MOBA_BLOCK), F32),
                            pltpu.VMEM((MOBA_BLOCK, MOBA_BLOCK), F32)]),
        compiler_params=_params("parallel", "parallel"),
        name="moba_prompt",
    )(jnp.asarray(own_tab), jnp.asarray(past_tab), qt, ka.reshape(batch, seq_len, D_AUG), va,
      km.reshape(batch, nb, D_AUG), slopes)


def _head_rows(t):
    r = lax.broadcasted_iota(jnp.int32, (N_HEADS * t, 1), 0)
    return r // t, r % t


def _row_slopes(t):
    head, _ = _head_rows(t)
    slope = jnp.zeros((N_HEADS * t, 1), F32)
    for hh in range(N_HEADS):
        slope = jnp.where(head == hh, ALIBI_SLOPES[hh], slope)
    return slope


def _skeys_kernel(pt_ref, q_ref, kn_ref, *rest, t, nblocks, past_len):
    del pt_ref
    pages = rest[:PAGES_PER_STEP]
    pp_ref, po_ref, inv_ref, qbd_ref, s_ref, g_ref = rest[PAGES_PER_STEP:]
    rows = N_HEADS * t
    c = pl.program_id(1)
    blocks_per_step = PAGES_PER_STEP // 2

    @pl.when(c == 0)
    def _():
        qt = jnp.concatenate([q_ref[...]] * N_HEADS, axis=0)
        head = lax.broadcasted_iota(jnp.int32, (rows, D_ATTN), 0) // t
        lane_head = lax.broadcasted_iota(jnp.int32, (rows, D_ATTN), 1) // HEAD_DIM
        qbd_ref[...] = jnp.where(head == lane_head, qt, 0.0).astype(BF16)
        g_ref[...] = jnp.zeros(g_ref.shape, F32)

    qbd = qbd_ref[...]
    g_lane = lax.broadcasted_iota(jnp.int32, (rows, LANES), 1)
    gate = g_ref[...]
    for bl in range(blocks_per_step):
        kt = jnp.concatenate([pages[2 * bl][...], pages[2 * bl + 1][...]], axis=1)
        j = c * blocks_per_step + bl
        s = _dot(qbd, kt.astype(BF16))
        s_ref[j] = s
        gate = jnp.where(g_lane == j, jnp.sum(s, axis=1, keepdims=True) * (1.0 / MOBA_BLOCK), gate)
    g_ref[...] = gate

    @pl.when(c == pl.num_programs(1) - 1)
    def _():
        rank = jnp.zeros(gate.shape, F32)
        for kk in range(nblocks):
            gk = gate[:, kk:kk + 1]
            rank = rank + jnp.where((gk > gate) | ((gk == gate) & (kk < g_lane)), 1.0, 0.0)
        unsel = jnp.where(rank < MOBA_TOPK, 0.0, NEG)
        slope = _row_slopes(t)
        _, qq = _head_rows(t)
        within = slope * lax.broadcasted_iota(jnp.int32, (rows, MOBA_BLOCK), 1).astype(F32)
        kn = jnp.concatenate([kn_ref[...], jnp.zeros((LANES - t, D_ATTN), F32)], axis=0).astype(BF16)
        r_own = lax.broadcasted_iota(jnp.int32, (rows, LANES), 1)
        s_own = jnp.where(r_own <= qq, _nt_dot(qbd, kn) + within[:, :LANES], NEG)
        m_run = jnp.full((rows, MOBA_BLOCK), NEG, F32)
        for j in range(nblocks):
            sj = s_ref[j] + within + (slope * float(j * MOBA_BLOCK - past_len) + unsel[:, j:j + 1])
            s_ref[j] = sj
            m_run = jnp.maximum(m_run, sj)
        m = jnp.maximum(jnp.max(m_run, axis=1, keepdims=True), jnp.max(s_own, axis=1, keepdims=True))
        p_own = jnp.exp(s_own - m)
        po_ref[...] = p_own.astype(po_ref.dtype)
        l_run = jnp.zeros((rows, MOBA_BLOCK), F32)
        for j in range(nblocks):
            pj = jnp.exp(s_ref[j] - m)
            pp_ref[j] = pj.astype(pp_ref.dtype)
            l_run = l_run + pj
        l = jnp.sum(l_run, axis=1, keepdims=True) + jnp.sum(p_own, axis=1, keepdims=True)
        inv_ref[...] = jnp.broadcast_to(1.0 / l, inv_ref.shape)


def _svals_kernel(pt_ref, pp_ref, po_ref, inv_ref, vn_ref, *rest, t):
    del pt_ref
    pages = rest[:PAGES_PER_STEP]
    o_ref, acc_ref = rest[PAGES_PER_STEP:]
    c = pl.program_id(1)

    @pl.when(c == 0)
    def _():
        vn = jnp.concatenate([vn_ref[...], jnp.zeros((LANES - t, D_ATTN), F32)], axis=0).astype(BF16)
        acc_ref[...] = _dot(po_ref[...], vn)

    acc = acc_ref[...]
    for bl in range(PAGES_PER_STEP // 2):
        vt = jnp.concatenate([pages[2 * bl][...], pages[2 * bl + 1][...]], axis=1).astype(BF16)
        acc = acc + _nt_dot(pp_ref[bl], vt)
    acc_ref[...] = acc

    @pl.when(c == pl.num_programs(1) - 1)
    def _():
        scaled = acc * inv_ref[:, 0:1]
        lane_head = lax.broadcasted_iota(jnp.int32, (t, D_ATTN), 1) // HEAD_DIM
        out = jnp.zeros((t, D_ATTN), F32)
        for hh in range(N_HEADS):
            out = jnp.where(lane_head == hh, scaled[hh * t:(hh + 1) * t, :], out)
        o_ref[...] = out


def _moba_sample(q, kn, vn, cache_k, cache_v, page_table, t):
    db, n_pages = page_table.shape
    page = cache_k.shape[2]
    past_len = n_pages * page
    assert page == LANES and 2 * page == MOBA_BLOCK and n_pages % PAGES_PER_STEP == 0
    assert t == SUBLANES and past_len % MOBA_BLOCK == 0
    nblocks = past_len // MOBA_BLOCK
    assert nblocks <= LANES
    steps = n_pages // PAGES_PER_STEP
    bps = PAGES_PER_STEP // 2
    rows = N_HEADS * t
    pt = page_table.reshape(-1).astype(jnp.int32)

    def page_spec(i):
        return pl.BlockSpec((None, D_ATTN, page),
                            lambda b, c, pt_ref: (pt_ref[b * n_pages + c * PAGES_PER_STEP + i], 0, 0))

    new_spec = pl.BlockSpec((t, D_ATTN), lambda b, c, pt_ref: (b, 0))
    row_spec = pl.BlockSpec((None, rows, LANES), lambda b, c, pt_ref: (b, 0, 0))
    pp, po, inv = pl.pallas_call(
        functools.partial(_skeys_kernel, t=t, nblocks=nblocks, past_len=past_len),
        out_shape=[jax.ShapeDtypeStruct((db, nblocks, rows, MOBA_BLOCK), BF16),
                   jax.ShapeDtypeStruct((db, rows, LANES), BF16),
                   jax.ShapeDtypeStruct((db, rows, LANES), F32)],
        grid_spec=pltpu.PrefetchScalarGridSpec(
            num_scalar_prefetch=1, grid=(db, steps),
            in_specs=[new_spec, new_spec] + [page_spec(i) for i in range(PAGES_PER_STEP)],
            out_specs=[pl.BlockSpec((None, nblocks, rows, MOBA_BLOCK), lambda b, c, pt_ref: (b, 0, 0, 0)),
                       row_spec, row_spec],
            scratch_shapes=[pltpu.VMEM((rows, D_ATTN), BF16),
                            pltpu.VMEM((nblocks, rows, MOBA_BLOCK), F32),
                            pltpu.VMEM((rows, LANES), F32)]),
        compiler_params=_params("parallel", "arbitrary"),
        name="sample_keys",
    )(pt, q, kn, *([cache_k] * PAGES_PER_STEP))
    return pl.pallas_call(
        functools.partial(_svals_kernel, t=t),
        out_shape=jax.ShapeDtypeStruct((db * t, D_ATTN), F32),
        grid_spec=pltpu.PrefetchScalarGridSpec(
            num_scalar_prefetch=1, grid=(db, steps),
            in_specs=[pl.BlockSpec((None, bps, rows, MOBA_BLOCK), lambda b, c, pt_ref: (b, c, 0, 0)),
                      row_spec, row_spec, new_spec] + [page_spec(i) for i in range(PAGES_PER_STEP)],
            out_specs=new_spec,
            scratch_shapes=[pltpu.VMEM((rows, D_ATTN), F32)]),
        compiler_params=_params("parallel", "arbitrary"),
        name="sample_values",
    )(pt, pp, po, inv, vn, *([cache_v] * PAGES_PER_STEP))


def _merge_kernel(o_ref, sa_ref, u_ref, up_ref, sb_ref, ga_ref, gb_ref, x_ref,
                  wp_ref, ps_ref, wba_ref, wbp_ref, wo_ref, gp_ref, y_ref, ubuf_ref,
                  *, ns, tm, t0, zero_first_prev):
    i = pl.program_id(1)
    n = ns * tm
    prev = up_ref[...]
    if zero_first_prev:
        prev = jnp.where(i == 0, 0.0, prev)
    ubuf_ref[:, 0:PREV_ROWS, :] = prev
    ubuf_ref[:, PREV_ROWS:PREV_ROWS + tm, :] = u_ref[...]
    pos = t0 + i * tm + lax.broadcasted_iota(jnp.int32, (ns, tm, POOL_GROUP), 1)
    mixed = []
    for gidx, w in enumerate(POOL_WINDOWS):
        gl = slice(gidx * POOL_GROUP, (gidx + 1) * POOL_GROUP)
        cur = ubuf_ref[:, PREV_ROWS:PREV_ROWS + tm, gl]
        tot = cur
        for s in range(1, w):
            tot = tot + ubuf_ref[:, PREV_ROWS - s:PREV_ROWS - s + tm, gl]
        cnt = jnp.minimum(pos + 1, w).astype(F32)
        pooled = (tot / cnt - cur).reshape(n, POOL_GROUP)
        mixed.append(_dot(pooled.astype(BF16), wp_ref[gidx]))
    mixed = jnp.concatenate(mixed, axis=1) * ps_ref[...]

    def flat(ref):
        return ref[...].reshape(n, ref.shape[-1]).astype(F32)

    o_b = (mixed * flat(sb_ref)).astype(BF16)
    o_a = (flat(o_ref) * flat(sa_ref)).astype(BF16)
    m_a = _dot(o_a, wba_ref[...])
    m_b = _dot(o_b, wbp_ref[...])
    merged = flat(ga_ref) * m_a + flat(gb_ref) * m_b
    z = _dot(merged.astype(BF16), wo_ref[...])
    ms = jnp.mean(z * z, axis=-1, keepdims=True)
    y = flat(x_ref) + z * lax.rsqrt(ms + RMS_EPS) * gp_ref[...]
    y_ref[...] = y.reshape(y_ref.shape)


def _merge(o, sa, u, u_prev, sb, ga, gb, x, ws, *, ns, tm, t0, zero_first_prev):
    nseq, seq_len, d_model = x.shape
    assert nseq % ns == 0 and seq_len % tm == 0 and tm % SUBLANES == 0
    assert ns == 1 or (tm == SUBLANES and all(a.dtype == F32 for a in (o, sa, sb, ga, gb)))
    tile = lambda width: pl.BlockSpec((ns, tm, width), lambda b, i: (b, i, 0))
    full = lambda a: pl.BlockSpec(a.shape, lambda b, i: (0,) * a.ndim)
    per = tm // PREV_ROWS if tm >= PREV_ROWS else 0
    prev_spec = pl.BlockSpec((ns, PREV_ROWS, D_POOL),
                             lambda b, i: (b, jnp.maximum(i * per - 1, 0), 0))
    kern = functools.partial(_merge_kernel, ns=ns, tm=tm, t0=t0, zero_first_prev=zero_first_prev)
    return pl.pallas_call(
        kern, out_shape=jax.ShapeDtypeStruct(x.shape, F32),
        grid=(nseq // ns, seq_len // tm),
        in_specs=[tile(D_ATTN), tile(D_ATTN), tile(D_POOL), prev_spec, tile(D_POOL),
                  tile(d_model), tile(d_model), tile(d_model)] + [full(w) for w in ws],
        out_specs=tile(d_model),
        scratch_shapes=[pltpu.VMEM((ns, PREV_ROWS + tm, D_POOL), F32)],
        compiler_params=_params("parallel", "arbitrary"),
        name="merge_prompt" if ns == 1 else "merge_sample",
    )(o, sa, u, u_prev, sb, ga, gb, x, *ws)


def _layer(yp, ys, cache_k, cache_v, state, page_table, g_pre, w_in, w_pool, pool_scale,
           w_br_attn, w_br_pool, w_out, g_post):
    batch, seq_len, d_model = yp.shape
    db, t, _ = ys.shape
    past_len = page_table.shape[1] * cache_k.shape[2]

    bounds = [0]
    for width in (D_ATTN, D_ATTN, D_ATTN, D_ATTN, D_POOL, D_POOL, d_model, d_model):
        bounds.append(bounds[-1] + width)
    wq, wk, wv, wza, wu, wzb, wga, wgb = [w_in[:, a:b] for a, b in zip(bounds[:-1], bounds[1:])]
    proj_ws = [w.astype(BF16) for w in (wq * HEAD_DIM ** -0.5, wk, wv, wza, wu, wzb, wga, wgb)]
    g_pre2 = g_pre.reshape(1, d_model)
    merge_ws = [w_pool.astype(BF16), pool_scale.reshape(1, D_POOL), w_br_attn.astype(BF16),
                w_br_pool.astype(BF16), w_out.astype(BF16), g_post.reshape(1, d_model)]

    kt, vt, up, qt, ka, va, sa, sb, ga, gb, km = _project(
        yp.reshape(batch * seq_len, d_model), g_pre2, proj_ws, prompt=True, seq_len=seq_len)
    op = _moba_prompt(qt, ka, va, km, batch, seq_len)
    r3 = lambda a: a.reshape(batch, seq_len, a.shape[-1])
    up3 = r3(up)
    y_prompt = _merge(op, r3(sa), up3, up3, r3(sb), r3(ga), r3(gb), yp, merge_ws,
                      ns=1, tm=512, t0=0, zero_first_prev=True)
    seq_major = lambda a: jnp.transpose(a.reshape(batch, N_HEADS, HEAD_DIM, seq_len), (0, 3, 1, 2))

    ks, vs, us, qs, sas, sbs, gas, gbs = _project(
        ys.reshape(db * t, d_model), g_pre2, proj_ws, prompt=False, seq_len=t)
    osamp = _moba_sample(qs, ks, vs, cache_k, cache_v, page_table, t)
    s3 = lambda a: a.reshape(db, t, a.shape[-1])
    us3 = s3(us)
    state16 = jnp.concatenate([jnp.zeros((db, PREV_ROWS - POOL_STATE, D_POOL), F32), state], axis=1)
    y_sample = _merge(s3(osamp), s3(sas), us3, state16, s3(sbs), s3(gas), s3(gbs), ys, merge_ws,
                      ns=16, tm=t, t0=past_len, zero_first_prev=False)

    pool_prompt = up3[:, seq_len - POOL_STATE:]
    pool_sample = jnp.concatenate([state, us3], axis=1)[:, -POOL_STATE:]
    return (y_prompt, y_sample, seq_major(kt), seq_major(vt), pool_prompt,
            ks.reshape(db, t, N_HEADS, HEAD_DIM), vs.reshape(db, t, N_HEADS, HEAD_DIM), pool_sample)


def kernel(x_prompt, x_sample, cache_k, cache_v, state_pool, page_table, g_pre, w_in, w_pool,
           pool_scale, w_br_attn, w_br_pool, w_out, g_post):
    depth = w_in.shape[0]
    n_phys, page = cache_k.shape[1], cache_k.shape[2]
    pages_t = lambda c: jnp.transpose(c, (0, 2, 3, 1)).reshape(n_phys, D_ATTN, page)
    yp, ys = x_prompt, x_sample
    per_layer = []
    for l in range(depth):
        outs = _layer(yp, ys, pages_t(cache_k[l]), pages_t(cache_v[l]),
                      state_pool[l], page_table, g_pre[l], w_in[l], w_pool[l], pool_scale[l],
                      w_br_attn[l], w_br_pool[l], w_out[l], g_post[l])
        yp, ys = outs[0], outs[1]
        per_layer.append(outs[2:])
    stacked = [jnp.stack([layer[i] for layer in per_layer]) for i in range(6)]
    return (yp, ys, *stacked)
```

```python
import functools

import numpy as np
import jax
import jax.numpy as jnp
from jax import lax
from jax.experimental import pallas as pl
from jax.experimental.pallas import tpu as pltpu

F32 = jnp.float32
BF16 = jnp.bfloat16

N_HEADS = 8
HEAD_DIM = 64
D_ATTN = N_HEADS * HEAD_DIM
MOBA_BLOCK = 256
MOBA_TOPK = 3
POOL_WINDOWS = (2, 4, 8, 16)
POOL_GROUP = 128
D_POOL = POOL_GROUP * len(POOL_WINDOWS)
POOL_STATE = max(POOL_WINDOWS) - 1
PREV_ROWS = 16
RMS_EPS = 1e-6
LANES = 128
SUBLANES = 8
BF16_ROWS = 16
NEG = -1e30
ALIBI_SLOPES = tuple(2.0 ** (-(h + 1)) for h in range(N_HEADS))

AUX_BIAS = HEAD_DIM
MAX_BLOCKS = 16
AUX_ONE = AUX_BIAS + MAX_BLOCKS
D_AUG = N_HEADS * LANES
V_ROWS = HEAD_DIM + BF16_ROWS
D_VT = N_HEADS * V_ROWS

MOBA_HEADS = 4
MOBA_LAG = 2
MOBA_RING = 2 * MOBA_LAG
MOBA_UNROLL = MOBA_RING
VMEM_LIMIT = 56 * 1024 * 1024
PROJ_BLOCKS = 2
PAGES_PER_STEP = 32


def _nt_dot(a, b):
    return lax.dot_general(a, b, (((1,), (1,)), ((), ())), preferred_element_type=F32)


def _dot(a, b):
    return jnp.dot(a, b, preferred_element_type=F32)


def _sigmoid(z):
    return 1.0 / (1.0 + jnp.exp(-z))


def _params(*semantics):
    return pltpu.CompilerParams(dimension_semantics=semantics, vmem_limit_bytes=VMEM_LIMIT)


def _proj_kernel(x_ref, g_ref, wq_ref, wk_ref, wv_ref, wza_ref, wu_ref, wzb_ref, wga_ref, wgb_ref,
                 *outs, prompt, blocks_per_seq):
    x = x_ref[...]
    ms = jnp.mean(x * x, axis=-1, keepdims=True)
    h = (x * lax.rsqrt(ms + RMS_EPS) * g_ref[...]).astype(BF16)

    def proj(w_ref):
        return _dot(h, w_ref[...])

    q = proj(wq_ref)
    k = proj(wk_ref)
    v = proj(wv_ref)
    za = proj(wza_ref)
    zb = proj(wzb_ref)
    if prompt:
        kt_ref, vt_ref, u_ref, qt_ref, ka_ref, va_ref, sa_ref, sb_ref, ga_ref, gb_ref, km_ref = outs
    else:
        k_ref, v_ref, u_ref, q_ref, sa_ref, sb_ref, ga_ref, gb_ref = outs
        q_ref[...] = q
        k_ref[...] = k
        v_ref[...] = v
    u_ref[...] = proj(wu_ref)
    sa_ref[...] = (za * _sigmoid(za)).astype(sa_ref.dtype)
    sb_ref[...] = (zb * _sigmoid(zb)).astype(sb_ref.dtype)
    ga_ref[...] = _sigmoid(proj(wga_ref)).astype(ga_ref.dtype)
    gb_ref[...] = _sigmoid(proj(wgb_ref)).astype(gb_ref.dtype)
    if not prompt:
        return

    v_t = v.T
    q_t = q.T.astype(BF16)
    kt_ref[...] = k.T
    vt_ref[...] = v_t
    ones_rows = jnp.where(lax.broadcasted_iota(jnp.int32, (BF16_ROWS, MOBA_BLOCK), 0) == 0, 1.0, 0.0)
    lane = lax.broadcasted_iota(jnp.int32, (MOBA_BLOCK, LANES), 1)
    kl = lax.broadcasted_iota(jnp.int32, (MOBA_BLOCK, LANES), 0).astype(F32)
    nblk = qt_ref.shape[0]
    for bl in range(nblk):
        rows = slice(bl * MOBA_BLOCK, (bl + 1) * MOBA_BLOCK)
        qt_ref[bl] = q_t[:, rows]
        pieces = []
        for hh in range(N_HEADS):
            pieces += [v_t[hh * HEAD_DIM:(hh + 1) * HEAD_DIM, rows], ones_rows]
        va_ref[bl] = jnp.concatenate(pieces, axis=0).astype(BF16)
        blk_in_seq = (pl.program_id(0) * nblk + bl) % blocks_per_seq
        onehot = jnp.where(lane - AUX_BIAS == blk_in_seq, 1.0, 0.0)
        for hh in range(N_HEADS):
            pair = k[rows, (hh // 2) * LANES:(hh // 2 + 1) * LANES]
            kd = pair if hh % 2 == 0 else pltpu.roll(pair, HEAD_DIM, 1)
            k_aux = jnp.where(lane == AUX_ONE, ALIBI_SLOPES[hh] * kl, onehot)
            dst = slice(hh * LANES, (hh + 1) * LANES)
            ka_ref[rows, dst] = jnp.where(lane < HEAD_DIM, kd, k_aux).astype(BF16)
            km_ref[bl, :, dst] = jnp.mean(kd, axis=0, keepdims=True)


def _project(x2, g, ws, *, prompt, seq_len):
    rows, d_model = x2.shape
    nblk = PROJ_BLOCKS if prompt else 1
    tm = nblk * MOBA_BLOCK
    assert rows % tm == 0
    inter = BF16 if prompt else F32
    row_spec = lambda width: pl.BlockSpec((tm, width), lambda i: (i, 0))
    full = lambda a: pl.BlockSpec(a.shape, lambda i: (0,) * a.ndim, pipeline_mode=pl.Buffered(1))
    sds = jax.ShapeDtypeStruct
    if prompt:
        assert seq_len % tm == 0 and seq_len // MOBA_BLOCK <= MAX_BLOCKS
        nb = seq_len // MOBA_BLOCK
        tps = seq_len // tm
        batch = rows // seq_len
        t_spec = pl.BlockSpec((None, D_ATTN, tm), lambda i: (i // tps, 0, i % tps))
        blk_spec = lambda r: pl.BlockSpec((None, nblk, r, MOBA_BLOCK), lambda i: (i // tps, i % tps, 0, 0))
        shapes = [sds((batch, D_ATTN, seq_len), F32), sds((batch, D_ATTN, seq_len), F32),
                  sds((rows, D_POOL), F32),
                  sds((batch, nb, D_ATTN, MOBA_BLOCK), BF16),
                  sds((rows, D_AUG), BF16),
                  sds((batch, nb, D_VT, MOBA_BLOCK), BF16)]
        specs = [t_spec, t_spec, row_spec(D_POOL), blk_spec(D_ATTN), row_spec(D_AUG), blk_spec(D_VT)]
    else:
        shapes = [sds((rows, D_ATTN), F32), sds((rows, D_ATTN), F32), sds((rows, D_POOL), F32),
                  sds((rows, D_ATTN), F32)]
        specs = [row_spec(D_ATTN), row_spec(D_ATTN), row_spec(D_POOL), row_spec(D_ATTN)]
    shapes += [sds((rows, D_ATTN), inter), sds((rows, D_POOL), inter),
               sds((rows, d_model), inter), sds((rows, d_model), inter)]
    specs += [row_spec(D_ATTN), row_spec(D_POOL), row_spec(d_model), row_spec(d_model)]
    if prompt:
        shapes += [sds((rows // MOBA_BLOCK, 1, D_AUG), F32)]
        specs += [pl.BlockSpec((nblk, 1, D_AUG), lambda i: (i, 0, 0))]
    kern = functools.partial(_proj_kernel, prompt=prompt,
                             blocks_per_seq=(seq_len // MOBA_BLOCK) if prompt else 1)
    return pl.pallas_call(
        kern, out_shape=shapes, grid=(rows // tm,),
        in_specs=[row_spec(d_model), full(g)] + [full(w) for w in ws],
        out_specs=specs, compiler_params=_params("parallel"),
        name="proj_prompt" if prompt else "proj_sample",
    )(x2, g, *ws)


def _select_bias(g, slope, i, nb):
    jidx = lax.broadcasted_iota(jnp.int32, g.shape, 0)
    own = jnp.zeros(g.shape, jnp.int32) + i
    rank = jnp.zeros(g.shape, F32)
    for kk in range(nb):
        gk = g[kk:kk + 1, :]
        beats = (gk > g) | ((gk == g) & (kk < jidx))
        rank = rank + jnp.where(beats & (own > kk), 1.0, 0.0)
    sel = (rank < MOBA_TOPK) & (jidx < own)
    rel = ((jidx - own) * MOBA_BLOCK).astype(F32) * slope
    return jnp.where(jidx == own, 0.0, jnp.where(sel, rel, NEG))


def _moba_schedule(items, nb):
    n, lag = len(items), MOBA_LAG
    steps = -(-(n + 2 * lag) // MOBA_UNROLL) * MOBA_UNROLL
    tab = np.zeros((5, steps), np.int32)
    for tau in range(steps):
        tab[1, tau], tab[0, tau] = items[tau] if tau < n else (nb - 1, 0)
        tab[2, tau] = items[tau - lag][0] if lag <= tau < n + lag else nb
        tab[4, tau], tab[3, tau] = items[tau - 2 * lag] if 2 * lag <= tau < n + 2 * lag else (nb, 0)
    return tab, steps


def _moba_kernel(own_ref, past_ref, qt_ref, k_ref, vt_ref, km_ref, al_ref, o_ref,
                 qs_ref, s_ref, p_ref, a_ref, m_ref, acc_ref, cb_ref, *, nb, own_steps, past_steps):
    tq = MOBA_BLOCK
    key_pos = lax.broadcasted_iota(jnp.int32, (MOBA_BLOCK, tq), 0)
    qry_pos = lax.broadcasted_iota(jnp.int32, (MOBA_BLOCK, tq), 1)
    cb_ref[...] = jnp.where(key_pos <= qry_pos, 0.0, NEG)
    for ref in (s_ref, p_ref, a_ref):
        ref[...] = jnp.zeros(ref.shape, ref.dtype)
    m_ref[nb] = jnp.zeros(m_ref.shape[1:], F32)
    acc_ref[nb] = jnp.zeros(acc_ref.shape[1:], F32)

    km = km_ref[...].astype(BF16)
    one_row = jnp.where(lax.broadcasted_iota(jnp.int32, (BF16_ROWS, tq), 0) == 0, 1.0, 0.0).astype(BF16)
    pad_rows = jnp.zeros((LANES - AUX_ONE - BF16_ROWS, tq), BF16)

    def gating(i, carry):
        qt = qt_ref[i]
        for e in range(MOBA_HEADS):
            qe = qt[e * HEAD_DIM:(e + 1) * HEAD_DIM]
            gate = _dot(km[:, e * LANES:e * LANES + HEAD_DIM], qe)
            slope = jnp.concatenate([al_ref[e:e + 1, :]] * (tq // LANES), axis=1)
            bias = _select_bias(gate, slope, i, nb)
            if nb < MAX_BLOCKS:
                bias = jnp.concatenate([bias, jnp.zeros((MAX_BLOCKS - nb, tq), F32)], axis=0)
            qs_ref[i, e] = jnp.concatenate([qe, bias.astype(BF16), one_row, pad_rows], axis=0)
        return carry

    lax.fori_loop(0, nb, gating, 0)

    def value_stage(tab_ref, tau, u, own):
        ring = u % MOBA_RING
        vblk = tab_ref[3, tau]
        slot = tab_ref[4, tau]
        for e in range(MOBA_HEADS):
            pv = _dot(vt_ref[vblk, e * V_ROWS:(e + 1) * V_ROWS, :], p_ref[ring, e])
            acc_ref[slot, e] = pv if own else a_ref[ring, e] * acc_ref[slot, e] + pv

    def softmax_stage(tab_ref, tau, u, own):
        ring = (u + MOBA_LAG) % MOBA_RING
        slot = tab_ref[2, tau]
        for e in range(MOBA_HEADS):
            s = s_ref[ring, e]
            m_new = jnp.max(s, axis=0, keepdims=True)
            if not own:
                m_prev = m_ref[slot, e]
                m_new = jnp.maximum(m_prev, m_new)
                a_ref[ring, e] = jnp.exp(m_prev - m_new)
            p_ref[ring, e] = jnp.exp(s - m_new).astype(BF16)
            m_ref[slot, e] = m_new

    def score_stage(tab_ref, tau, u, own):
        ring = u % MOBA_RING
        r0 = pl.multiple_of(tab_ref[0, tau] * MOBA_BLOCK, MOBA_BLOCK)
        qblk = tab_ref[1, tau]
        for e in range(MOBA_HEADS):
            s = _dot(k_ref[pl.ds(r0, MOBA_BLOCK), e * LANES:(e + 1) * LANES], qs_ref[qblk, e])
            s_ref[ring, e] = s + cb_ref[...] if own else s

    def pipeline(tab_ref, steps, own):
        def body(it, carry):
            for u in range(MOBA_UNROLL):
                for stage in (score_stage, value_stage, softmax_stage):
                    stage(tab_ref, MOBA_UNROLL * it + u, u, own)
            return carry

        lax.fori_loop(0, steps // MOBA_UNROLL, body, 0)

    pipeline(own_ref, own_steps, True)
    pipeline(past_ref, past_steps, False)

    def finalize(i, carry):
        outs = []
        for e in range(MOBA_HEADS):
            acc = acc_ref[i, e]
            outs.append(acc[0:HEAD_DIM] / acc[HEAD_DIM:HEAD_DIM + 1])
        q0 = pl.multiple_of(i * tq, tq)
        o_ref[pl.ds(q0, tq), :] = jnp.concatenate(outs, axis=0).T.astype(o_ref.dtype)
        return carry

    lax.fori_loop(0, nb, finalize, 0)


def _moba_prompt(qt, ka, va, km, batch, seq_len):
    nb = seq_len // MOBA_BLOCK
    assert nb <= MAX_BLOCKS and N_HEADS % MOBA_HEADS == 0
    groups = N_HEADS // MOBA_HEADS
    own_tab, own_steps = _moba_schedule([(i, i) for i in range(nb)], nb)
    past_tab, past_steps = _moba_schedule([(i, j) for i in range(nb) for j in range(i)], nb)
    slopes = jnp.broadcast_to(jnp.asarray(ALIBI_SLOPES, F32).reshape(groups, MOBA_HEADS, 1),
                              (groups, MOBA_HEADS, LANES))
    blk_spec = lambda r: pl.BlockSpec((None, nb, r, MOBA_BLOCK), lambda b, p, *_: (b, 0, p, 0))
    seq_spec = lambda w: pl.BlockSpec((None, seq_len, w), lambda b, p, *_: (b, 0, p))
    return pl.pallas_call(
        functools.partial(_moba_kernel, nb=nb, own_steps=own_steps, past_steps=past_steps),
        out_shape=jax.ShapeDtypeStruct((batch, seq_len, D_ATTN), BF16),
        grid_spec=pltpu.PrefetchScalarGridSpec(
            num_scalar_prefetch=2, grid=(batch, groups),
            in_specs=[blk_spec(MOBA_HEADS * HEAD_DIM), seq_spec(MOBA_HEADS * LANES),
                      blk_spec(MOBA_HEADS * V_ROWS),
                      pl.BlockSpec((None, nb, MOBA_HEADS * LANES), lambda b, p, *_: (b, 0, p)),
                      pl.BlockSpec((None, MOBA_HEADS, LANES), lambda b, p, *_: (p, 0, 0))],
            out_specs=seq_spec(MOBA_HEADS * HEAD_DIM),
            scratch_shapes=[pltpu.VMEM((nb, MOBA_HEADS, LANES, MOBA_BLOCK), BF16),
                            pltpu.VMEM((MOBA_RING, MOBA_HEADS, MOBA_BLOCK, MOBA_BLOCK), F32),
                            pltpu.VMEM((MOBA_RING, MOBA_HEADS, MOBA_BLOCK, MOBA_BLOCK), BF16),
                            pltpu.VMEM((MOBA_RING, MOBA_HEADS, 1, MOBA_BLOCK), F32),
                            pltpu.VMEM((nb + 1, MOBA_HEADS, 1, MOBA_BLOCK), F32),
                            pltpu.VMEM((nb + 1, MOBA_HEADS, V_ROWS, MOBA_BLOCK), F32),
                            pltpu.VMEM((MOBA_BLOCK, MOBA_BLOCK), F32)]),
        compiler_params=_params("parallel", "parallel"),
        name="moba_prompt",
    )(jnp.asarray(own_tab), jnp.asarray(past_tab), qt, ka.reshape(batch, seq_len, D_AUG), va,
      km.reshape(batch, nb, D_AUG), slopes)


def _head_rows(t):
    r = lax.broadcasted_iota(jnp.int32, (N_HEADS * t, 1), 0)
    return r // t, r % t


def _row_slopes(t):
    head, _ = _head_rows(t)
    slope = jnp.zeros((N_HEADS * t, 1), F32)
    for hh in range(N_HEADS):
        slope = jnp.where(head == hh, ALIBI_SLOPES[hh], slope)
    return slope


def _skeys_kernel(pt_ref, q_ref, kn_ref, *rest, t, nblocks, past_len):
    del pt_ref
    pages = rest[:PAGES_PER_STEP]
    pp_ref, po_ref, inv_ref, qbd_ref, s_ref, g_ref = rest[PAGES_PER_STEP:]
    rows = N_HEADS * t
    c = pl.program_id(1)
    blocks_per_step = PAGES_PER_STEP // 2

    @pl.when(c == 0)
    def _():
        qt = jnp.concatenate([q_ref[...]] * N_HEADS, axis=0)
        head = lax.broadcasted_iota(jnp.int32, (rows, D_ATTN), 0) // t
        lane_head = lax.broadcasted_iota(jnp.int32, (rows, D_ATTN), 1) // HEAD_DIM
        qbd_ref[...] = jnp.where(head == lane_head, qt, 0.0).astype(BF16)
        g_ref[...] = jnp.zeros(g_ref.shape, F32)

    qbd = qbd_ref[...]
    g_lane = lax.broadcasted_iota(jnp.int32, (rows, LANES), 1)
    gate = g_ref[...]
    for bl in range(blocks_per_step):
        kt = jnp.concatenate([pages[2 * bl][...], pages[2 * bl + 1][...]], axis=1)
        j = c * blocks_per_step + bl
        s = _dot(qbd, kt.astype(BF16))
        s_ref[j] = s
        gate = jnp.where(g_lane == j, jnp.sum(s, axis=1, keepdims=True) * (1.0 / MOBA_BLOCK), gate)
    g_ref[...] = gate

    @pl.when(c == pl.num_programs(1) - 1)
    def _():
        rank = jnp.zeros(gate.shape, F32)
        for kk in range(nblocks):
            gk = gate[:, kk:kk + 1]
            rank = rank + jnp.where((gk > gate) | ((gk == gate) & (kk < g_lane)), 1.0, 0.0)
        unsel = jnp.where(rank < MOBA_TOPK, 0.0, NEG)
        slope = _row_slopes(t)
        _, qq = _head_rows(t)
        within = slope * lax.broadcasted_iota(jnp.int32, (rows, MOBA_BLOCK), 1).astype(F32)
        kn = jnp.concatenate([kn_ref[...], jnp.zeros((LANES - t, D_ATTN), F32)], axis=0).astype(BF16)
        r_own = lax.broadcasted_iota(jnp.int32, (rows, LANES), 1)
        s_own = jnp.where(r_own <= qq, _nt_dot(qbd, kn) + within[:, :LANES], NEG)
        m_run = jnp.full((rows, MOBA_BLOCK), NEG, F32)
        for j in range(nblocks):
            sj = s_ref[j] + within + (slope * float(j * MOBA_BLOCK - past_len) + unsel[:, j:j + 1])
            s_ref[j] = sj
            m_run = jnp.maximum(m_run, sj)
        m = jnp.maximum(jnp.max(m_run, axis=1, keepdims=True), jnp.max(s_own, axis=1, keepdims=True))
        p_own = jnp.exp(s_own - m)
        po_ref[...] = p_own.astype(po_ref.dtype)
        l_run = jnp.zeros((rows, MOBA_BLOCK), F32)
        for j in range(nblocks):
            pj = jnp.exp(s_ref[j] - m)
            pp_ref[j] = pj.astype(pp_ref.dtype)
            l_run = l_run + pj
        l = jnp.sum(l_run, axis=1, keepdims=True) + jnp.sum(p_own, axis=1, keepdims=True)
        inv_ref[...] = jnp.broadcast_to(1.0 / l, inv_ref.shape)


def _svals_kernel(pt_ref, pp_ref, po_ref, inv_ref, vn_ref, *rest, t):
    del pt_ref
    pages = rest[:PAGES_PER_STEP]
    o_ref, acc_ref = rest[PAGES_PER_STEP:]
    c = pl.program_id(1)

    @pl.when(c == 0)
    def _():
        vn = jnp.concatenate([vn_ref[...], jnp.zeros((LANES - t, D_ATTN), F32)], axis=0).astype(BF16)
        acc_ref[...] = _dot(po_ref[...], vn)

    acc = acc_ref[...]
    for bl in range(PAGES_PER_STEP // 2):
        vt = jnp.concatenate([pages[2 * bl][...], pages[2 * bl + 1][...]], axis=1).astype(BF16)
        acc = acc + _nt_dot(pp_ref[bl], vt)
    acc_ref[...] = acc

    @pl.when(c == pl.num_programs(1) - 1)
    def _():
        scaled = acc * inv_ref[:, 0:1]
        lane_head = lax.broadcasted_iota(jnp.int32, (t, D_ATTN), 1) // HEAD_DIM
        out = jnp.zeros((t, D_ATTN), F32)
        for hh in range(N_HEADS):
            out = jnp.where(lane_head == hh, scaled[hh * t:(hh + 1) * t, :], out)
        o_ref[...] = out


def _moba_sample(q, kn, vn, cache_k, cache_v, page_table, t):
    db, n_pages = page_table.shape
    page = cache_k.shape[2]
    past_len = n_pages * page
    assert page == LANES and 2 * page == MOBA_BLOCK and n_pages % PAGES_PER_STEP == 0
    assert t == SUBLANES and past_len % MOBA_BLOCK == 0
    nblocks = past_len // MOBA_BLOCK
    assert nblocks <= LANES
    steps = n_pages // PAGES_PER_STEP
    bps = PAGES_PER_STEP // 2
    rows = N_HEADS * t
    pt = page_table.reshape(-1).astype(jnp.int32)

    def page_spec(i):
        return pl.BlockSpec((None, D_ATTN, page),
                            lambda b, c, pt_ref: (pt_ref[b * n_pages + c * PAGES_PER_STEP + i], 0, 0))

    new_spec = pl.BlockSpec((t, D_ATTN), lambda b, c, pt_ref: (b, 0))
    row_spec = pl.BlockSpec((None, rows, LANES), lambda b, c, pt_ref: (b, 0, 0))
    pp, po, inv = pl.pallas_call(
        functools.partial(_skeys_kernel, t=t, nblocks=nblocks, past_len=past_len),
        out_shape=[jax.ShapeDtypeStruct((db, nblocks, rows, MOBA_BLOCK), BF16),
                   jax.ShapeDtypeStruct((db, rows, LANES), BF16),
                   jax.ShapeDtypeStruct((db, rows, LANES), F32)],
        grid_spec=pltpu.PrefetchScalarGridSpec(
            num_scalar_prefetch=1, grid=(db, steps),
            in_specs=[new_spec, new_spec] + [page_spec(i) for i in range(PAGES_PER_STEP)],
            out_specs=[pl.BlockSpec((None, nblocks, rows, MOBA_BLOCK), lambda b, c, pt_ref: (b, 0, 0, 0)),
                       row_spec, row_spec],
            scratch_shapes=[pltpu.VMEM((rows, D_ATTN), BF16),
                            pltpu.VMEM((nblocks, rows, MOBA_BLOCK), F32),
                            pltpu.VMEM((rows, LANES), F32)]),
        compiler_params=_params("parallel", "arbitrary"),
        name="sample_keys",
    )(pt, q, kn, *([cache_k] * PAGES_PER_STEP))
    return pl.pallas_call(
        functools.partial(_svals_kernel, t=t),
        out_shape=jax.ShapeDtypeStruct((db * t, D_ATTN), F32),
        grid_spec=pltpu.PrefetchScalarGridSpec(
            num_scalar_prefetch=1, grid=(db, steps),
            in_specs=[pl.BlockSpec((None, bps, rows, MOBA_BLOCK), lambda b, c, pt_ref: (b, c, 0, 0)),
                      row_spec, row_spec, new_spec] + [page_spec(i) for i in range(PAGES_PER_STEP)],
            out_specs=new_spec,
            scratch_shapes=[pltpu.VMEM((rows, D_ATTN), F32)]),
        compiler_params=_params("parallel", "arbitrary"),
        name="sample_values",
    )(pt, pp, po, inv, vn, *([cache_v] * PAGES_PER_STEP))


def _merge_kernel(o_ref, sa_ref, u_ref, up_ref, sb_ref, ga_ref, gb_ref, x_ref,
                  wp_ref, ps_ref, wba_ref, wbp_ref, wo_ref, gp_ref, y_ref, ubuf_ref,
                  *, ns, tm, t0, zero_first_prev):
    i = pl.program_id(1)
    n = ns * tm
    prev = up_ref[...]
    if zero_first_prev:
        prev = jnp.where(i == 0, 0.0, prev)
    ubuf_ref[:, 0:PREV_ROWS, :] = prev
    ubuf_ref[:, PREV_ROWS:PREV_ROWS + tm, :] = u_ref[...]
    pos = t0 + i * tm + lax.broadcasted_iota(jnp.int32, (ns, tm, POOL_GROUP), 1)
    mixed = []
    for gidx, w in enumerate(POOL_WINDOWS):
        gl = slice(gidx * POOL_GROUP, (gidx + 1) * POOL_GROUP)
        cur = ubuf_ref[:, PREV_ROWS:PREV_ROWS + tm, gl]
        tot = cur
        for s in range(1, w):
            tot = tot + ubuf_ref[:, PREV_ROWS - s:PREV_ROWS - s + tm, gl]
        cnt = jnp.minimum(pos + 1, w).astype(F32)
        pooled = (tot / cnt - cur).reshape(n, POOL_GROUP)
        mixed.append(_dot(pooled.astype(BF16), wp_ref[gidx]))
    mixed = jnp.concatenate(mixed, axis=1) * ps_ref[...]

    def flat(ref):
        return ref[...].reshape(n, ref.shape[-1]).astype(F32)

    o_b = (mixed * flat(sb_ref)).astype(BF16)
    o_a = (flat(o_ref) * flat(sa_ref)).astype(BF16)
    m_a = _dot(o_a, wba_ref[...])
    m_b = _dot(o_b, wbp_ref[...])
    merged = flat(ga_ref) * m_a + flat(gb_ref) * m_b
    z = _dot(merged.astype(BF16), wo_ref[...])
    ms = jnp.mean(z * z, axis=-1, keepdims=True)
    y = flat(x_ref) + z * lax.rsqrt(ms + RMS_EPS) * gp_ref[...]
    y_ref[...] = y.reshape(y_ref.shape)


def _merge(o, sa, u, u_prev, sb, ga, gb, x, ws, *, ns, tm, t0, zero_first_prev):
    nseq, seq_len, d_model = x.shape
    assert nseq % ns == 0 and seq_len % tm == 0 and tm % SUBLANES == 0
    assert ns == 1 or (tm == SUBLANES and all(a.dtype == F32 for a in (o, sa, sb, ga, gb)))
    tile = lambda width: pl.BlockSpec((ns, tm, width), lambda b, i: (b, i, 0))
    full = lambda a: pl.BlockSpec(a.shape, lambda b, i: (0,) * a.ndim, pipeline_mode=pl.Buffered(1))
    per = tm // PREV_ROWS if tm >= PREV_ROWS else 0
    prev_spec = pl.BlockSpec((ns, PREV_ROWS, D_POOL),
                             lambda b, i: (b, jnp.maximum(i * per - 1, 0), 0))
    kern = functools.partial(_merge_kernel, ns=ns, tm=tm, t0=t0, zero_first_prev=zero_first_prev)
    return pl.pallas_call(
        kern, out_shape=jax.ShapeDtypeStruct(x.shape, F32),
        grid=(nseq // ns, seq_len // tm),
        in_specs=[tile(D_ATTN), tile(D_ATTN), tile(D_POOL), prev_spec, tile(D_POOL),
                  tile(d_model), tile(d_model), tile(d_model)] + [full(w) for w in ws],
        out_specs=tile(d_model),
        scratch_shapes=[pltpu.VMEM((ns, PREV_ROWS + tm, D_POOL), F32)],
        compiler_params=_params("parallel", "arbitrary"),
        name="merge_prompt" if ns == 1 else "merge_sample",
    )(o, sa, u, u_prev, sb, ga, gb, x, *ws)


def _layer(yp, ys, cache_k, cache_v, state, page_table, g_pre, w_in, w_pool, pool_scale,
           w_br_attn, w_br_pool, w_out, g_post):
    batch, seq_len, d_model = yp.shape
    db, t, _ = ys.shape
    past_len = page_table.shape[1] * cache_k.shape[2]

    bounds = [0]
    for width in (D_ATTN, D_ATTN, D_ATTN, D_ATTN, D_POOL, D_POOL, d_model, d_model):
        bounds.append(bounds[-1] + width)
    wq, wk, wv, wza, wu, wzb, wga, wgb = [w_in[:, a:b] for a, b in zip(bounds[:-1], bounds[1:])]
    proj_ws = [w.astype(BF16) for w in (wq * HEAD_DIM ** -0.5, wk, wv, wza, wu, wzb, wga, wgb)]
    g_pre2 = g_pre.reshape(1, d_model)
    merge_ws = [w_pool.astype(BF16), pool_scale.reshape(1, D_POOL), w_br_attn.astype(BF16),
                w_br_pool.astype(BF16), w_out.astype(BF16), g_post.reshape(1, d_model)]

    kt, vt, up, qt, ka, va, sa, sb, ga, gb, km = _project(
        yp.reshape(batch * seq_len, d_model), g_pre2, proj_ws, prompt=True, seq_len=seq_len)
    op = _moba_prompt(qt, ka, va, km, batch, seq_len)
    r3 = lambda a: a.reshape(batch, seq_len, a.shape[-1])
    up3 = r3(up)
    y_prompt = _merge(op, r3(sa), up3, up3, r3(sb), r3(ga), r3(gb), yp, merge_ws,
                      ns=1, tm=512, t0=0, zero_first_prev=True)
    seq_major = lambda a: jnp.transpose(a.reshape(batch, N_HEADS, HEAD_DIM, seq_len), (0, 3, 1, 2))

    ks, vs, us, qs, sas, sbs, gas, gbs = _project(
        ys.reshape(db * t, d_model), g_pre2, proj_ws, prompt=False, seq_len=t)
    osamp = _moba_sample(qs, ks, vs, cache_k, cache_v, page_table, t)
    s3 = lambda a: a.reshape(db, t, a.shape[-1])
    us3 = s3(us)
    state16 = jnp.concatenate([jnp.zeros((db, PREV_ROWS - POOL_STATE, D_POOL), F32), state], axis=1)
    y_sample = _merge(s3(osamp), s3(sas), us3, state16, s3(sbs), s3(gas), s3(gbs), ys, merge_ws,
                      ns=16, tm=t, t0=past_len, zero_first_prev=False)

    pool_prompt = up3[:, seq_len - POOL_STATE:]
    pool_sample = jnp.concatenate([state, us3], axis=1)[:, -POOL_STATE:]
    return (y_prompt, y_sample, seq_major(kt), seq_major(vt), pool_prompt,
            ks.reshape(db, t, N_HEADS, HEAD_DIM), vs.reshape(db, t, N_HEADS, HEAD_DIM), pool_sample)


def kernel(x_prompt, x_sample, cache_k, cache_v, state_pool, page_table, g_pre, w_in, w_pool,
           pool_scale, w_br_attn, w_br_pool, w_out, g_post):
    depth = w_in.shape[0]
    n_phys, page = cache_k.shape[1], cache_k.shape[2]
    pages_t = lambda c: jnp.transpose(c, (0, 2, 3, 1)).reshape(n_phys, D_ATTN, page)
    yp, ys = x_prompt, x_sample
    per_layer = []
    for l in range(depth):
        outs = _layer(yp, ys, pages_t(cache_k[l]), pages_t(cache_v[l]),
                      state_pool[l], page_table, g_pre[l], w_in[l], w_pool[l], pool_scale[l],
                      w_br_attn[l], w_br_pool[l], w_out[l], g_post[l])
        yp, ys = outs[0], outs[1]
        per_layer.append(outs[2:])
    stacked = [jnp.stack([layer[i] for layer in per_layer]) for i in range(6)]
    return (yp, ys, *stacked)
```

```python
import collections
import functools

import numpy as np
import jax
import jax.numpy as jnp
from jax import lax
from jax.experimental import pallas as pl
from jax.experimental.pallas import tpu as pltpu

F32 = jnp.float32
BF16 = jnp.bfloat16

N_HEADS = 8
HEAD_DIM = 64
D_ATTN = N_HEADS * HEAD_DIM
MOBA_BLOCK = 256
MOBA_TOPK = 3
POOL_WINDOWS = (2, 4, 8, 16)
POOL_GROUP = 128
D_POOL = POOL_GROUP * len(POOL_WINDOWS)
POOL_STATE = max(POOL_WINDOWS) - 1
PREV_ROWS = 16
RMS_EPS = 1e-6
LANES = 128
SUBLANES = 8
BF16_ROWS = 16
NEG = -1e30
ALIBI_SLOPES = tuple(2.0 ** (-(h + 1)) for h in range(N_HEADS))

AUX_BIAS = HEAD_DIM
MAX_BLOCKS = 16
AUX_ONE = AUX_BIAS + MAX_BLOCKS
D_AUG = N_HEADS * LANES
V_ROWS = HEAD_DIM + BF16_ROWS
D_VT = N_HEADS * V_ROWS

MOBA_HEADS = 4
MOBA_LAG = 2
MOBA_RING = 2 * MOBA_LAG
MOBA_UNROLL = MOBA_RING
VMEM_LIMIT = 56 * 1024 * 1024
PROJ_BLOCKS = 1


def _nt_dot(a, b):
    return lax.dot_general(a, b, (((1,), (1,)), ((), ())), preferred_element_type=F32)


def _dot(a, b):
    return jnp.dot(a, b, preferred_element_type=F32)


def _sigmoid(z):
    return 1.0 / (1.0 + jnp.exp(-z))


def _params(*semantics):
    return pltpu.CompilerParams(dimension_semantics=semantics, vmem_limit_bytes=VMEM_LIMIT)


def _proj_kernel(x_ref, g_ref, wq_ref, wk_ref, wv_ref, wza_ref, wu_ref, wzb_ref, wga_ref, wgb_ref,
                 *outs, prompt, blocks_per_seq):
    x = x_ref[...]
    ms = jnp.mean(x * x, axis=-1, keepdims=True)
    h = (x * lax.rsqrt(ms + RMS_EPS) * g_ref[...]).astype(BF16)

    def proj(w_ref):
        return _dot(h, w_ref[...])

    q = proj(wq_ref)
    k = proj(wk_ref)
    v = proj(wv_ref)
    za = proj(wza_ref)
    zb = proj(wzb_ref)
    if prompt:
        kt_ref, vt_ref, u_ref, qt_ref, ka_ref, va_ref, sa_ref, sb_ref, ga_ref, gb_ref, km_ref = outs
    else:
        k_ref, v_ref, u_ref, q_ref, sa_ref, sb_ref, ga_ref, gb_ref = outs
        q_ref[...] = q
        k_ref[...] = k
        v_ref[...] = v
    u_ref[...] = proj(wu_ref)
    sa_ref[...] = (za * _sigmoid(za)).astype(sa_ref.dtype)
    sb_ref[...] = (zb * _sigmoid(zb)).astype(sb_ref.dtype)
    ga_ref[...] = _sigmoid(proj(wga_ref)).astype(ga_ref.dtype)
    gb_ref[...] = _sigmoid(proj(wgb_ref)).astype(gb_ref.dtype)
    if not prompt:
        return

    v_t = v.T
    q_t = q.T.astype(BF16)
    kt_ref[...] = k.T
    vt_ref[...] = v_t
    ones_rows = jnp.where(lax.broadcasted_iota(jnp.int32, (BF16_ROWS, MOBA_BLOCK), 0) == 0, 1.0, 0.0)
    lane = lax.broadcasted_iota(jnp.int32, (MOBA_BLOCK, LANES), 1)
    kl = lax.broadcasted_iota(jnp.int32, (MOBA_BLOCK, LANES), 0).astype(F32)
    nblk = qt_ref.shape[0]
    for bl in range(nblk):
        rows = slice(bl * MOBA_BLOCK, (bl + 1) * MOBA_BLOCK)
        qt_ref[bl] = q_t[:, rows]
        pieces = []
        for hh in range(N_HEADS):
            pieces += [v_t[hh * HEAD_DIM:(hh + 1) * HEAD_DIM, rows], ones_rows]
        va_ref[bl] = jnp.concatenate(pieces, axis=0).astype(BF16)
        blk_in_seq = (pl.program_id(0) * nblk + bl) % blocks_per_seq
        onehot = jnp.where(lane - AUX_BIAS == blk_in_seq, 1.0, 0.0)
        for hh in range(N_HEADS):
            pair = k[rows, (hh // 2) * LANES:(hh // 2 + 1) * LANES]
            kd = pair if hh % 2 == 0 else pltpu.roll(pair, HEAD_DIM, 1)
            k_aux = jnp.where(lane == AUX_ONE, ALIBI_SLOPES[hh] * kl, onehot)
            dst = slice(hh * LANES, (hh + 1) * LANES)
            ka_ref[rows, dst] = jnp.where(lane < HEAD_DIM, kd, k_aux).astype(BF16)
            km_ref[bl, :, dst] = jnp.mean(kd, axis=0, keepdims=True)


StreamGeometry = collections.namedtuple(
    "StreamGeometry", "db n_pages past_len nblocks rows t spp pps")


def _stream_geometry(page_table, cache, t, host_steps):
    db, n_pages = page_table.shape
    page = cache.shape[2]
    past_len = n_pages * page
    assert page == LANES and 2 * page == MOBA_BLOCK and past_len % MOBA_BLOCK == 0 and t == SUBLANES
    assert host_steps % db == 0 and n_pages % (host_steps // db) == 0
    spp = host_steps // db
    pps = n_pages // spp
    nblocks = past_len // MOBA_BLOCK
    assert pps % 2 == 0 and nblocks <= LANES
    return StreamGeometry(db, n_pages, past_len, nblocks, N_HEADS * t, t, spp, pps)


def _proj_keys_kernel(pt_ref, *refs, n_proj_out, blocks_per_seq, geo):
    del pt_ref
    n_in = 10
    proj_in, (q_ref, kn_ref) = refs[:n_in], refs[n_in:n_in + 2]
    pages = refs[n_in + 2:n_in + 2 + geo.pps]
    outs = refs[n_in + 2 + geo.pps:]
    proj_out, (pp_ref, po_ref, inv_ref, qbd_ref, s_ref, g_ref) = outs[:n_proj_out], outs[n_proj_out:]
    _proj_kernel(*proj_in, *proj_out, prompt=True, blocks_per_seq=blocks_per_seq)
    c = pl.program_id(0) % geo.spp
    _skeys_body(c, c == geo.spp - 1, q_ref, kn_ref, pages, pp_ref, po_ref, inv_ref, qbd_ref, s_ref, g_ref,
                t=geo.t, nblocks=geo.nblocks, past_len=geo.past_len)


def _project(x2, g, ws, *, prompt, seq_len, keys=None):
    rows, d_model = x2.shape
    nblk = PROJ_BLOCKS if prompt else 1
    tm = nblk * MOBA_BLOCK
    assert rows % tm == 0
    inter = BF16 if prompt else F32
    row_spec = lambda width: pl.BlockSpec((tm, width), lambda i, *_: (i, 0))
    full = lambda a: pl.BlockSpec(a.shape, lambda i, *_: (0,) * a.ndim, pipeline_mode=pl.Buffered(1))
    sds = jax.ShapeDtypeStruct
    if prompt:
        assert seq_len % tm == 0 and seq_len // MOBA_BLOCK <= MAX_BLOCKS
        nb = seq_len // MOBA_BLOCK
        tps = seq_len // tm
        batch = rows // seq_len
        t_spec = pl.BlockSpec((None, D_ATTN, tm), lambda i, *_: (i // tps, 0, i % tps))
        blk_spec = lambda r: pl.BlockSpec((None, nblk, r, MOBA_BLOCK), lambda i, *_: (i // tps, i % tps, 0, 0))
        shapes = [sds((batch, D_ATTN, seq_len), F32), sds((batch, D_ATTN, seq_len), F32),
                  sds((rows, D_POOL), F32),
                  sds((batch, nb, D_ATTN, MOBA_BLOCK), BF16),
                  sds((rows, D_AUG), BF16),
                  sds((batch, nb, D_VT, MOBA_BLOCK), BF16)]
        specs = [t_spec, t_spec, row_spec(D_POOL), blk_spec(D_ATTN), row_spec(D_AUG), blk_spec(D_VT)]
    else:
        shapes = [sds((rows, D_ATTN), F32), sds((rows, D_ATTN), F32), sds((rows, D_POOL), F32),
                  sds((rows, D_ATTN), F32)]
        specs = [row_spec(D_ATTN), row_spec(D_ATTN), row_spec(D_POOL), row_spec(D_ATTN)]
    shapes += [sds((rows, D_ATTN), inter), sds((rows, D_POOL), inter),
               sds((rows, d_model), inter), sds((rows, d_model), inter)]
    specs += [row_spec(D_ATTN), row_spec(D_POOL), row_spec(d_model), row_spec(d_model)]
    if prompt:
        shapes += [sds((rows // MOBA_BLOCK, 1, D_AUG), F32)]
        specs += [pl.BlockSpec((nblk, 1, D_AUG), lambda i, *_: (i, 0, 0))]
    blocks_per_seq = (seq_len // MOBA_BLOCK) if prompt else 1
    in_specs = [row_spec(d_model), full(g)] + [full(w) for w in ws]
    if keys is None:
        return pl.pallas_call(
            functools.partial(_proj_kernel, prompt=prompt, blocks_per_seq=blocks_per_seq),
            out_shape=shapes, grid=(rows // tm,), in_specs=in_specs, out_specs=specs,
            compiler_params=_params("parallel"), name="proj_prompt" if prompt else "proj_sample",
        )(x2, g, *ws)

    q_s, kn_s, cache_k, page_table, t = keys
    geo = _stream_geometry(page_table, cache_k, t, rows // tm)
    seq_of = lambda i: i // geo.spp
    new_spec = pl.BlockSpec((t, D_ATTN), lambda i, pt_ref: (seq_of(i), 0))
    per_seq = pl.BlockSpec((None, geo.rows, LANES), lambda i, pt_ref: (seq_of(i), 0, 0))
    page_spec = lambda n: pl.BlockSpec(
        (None, D_ATTN, LANES),
        lambda i, pt_ref: (pt_ref[seq_of(i) * geo.n_pages + (i % geo.spp) * geo.pps + n], 0, 0))
    outs = pl.pallas_call(
        functools.partial(_proj_keys_kernel, n_proj_out=len(shapes), blocks_per_seq=blocks_per_seq, geo=geo),
        out_shape=shapes + [sds((geo.db, geo.nblocks, geo.rows, MOBA_BLOCK), BF16),
                            sds((geo.db, geo.rows, LANES), BF16), sds((geo.db, geo.rows, LANES), F32)],
        grid_spec=pltpu.PrefetchScalarGridSpec(
            num_scalar_prefetch=1, grid=(rows // tm,),
            in_specs=in_specs + [new_spec, new_spec] + [page_spec(n) for n in range(geo.pps)],
            out_specs=specs + [pl.BlockSpec((None, geo.nblocks, geo.rows, MOBA_BLOCK),
                                            lambda i, pt_ref: (seq_of(i), 0, 0, 0)), per_seq, per_seq],
            scratch_shapes=[pltpu.VMEM((geo.rows, D_ATTN), BF16),
                            pltpu.VMEM((geo.nblocks, geo.rows, MOBA_BLOCK), F32),
                            pltpu.VMEM((geo.rows, LANES), F32)]),
        compiler_params=_params("arbitrary"), name="proj_prompt_keys",
    )(page_table.reshape(-1).astype(jnp.int32), x2, g, *ws, q_s, kn_s, *([cache_k] * geo.pps))
    return outs[:len(shapes)], outs[len(shapes):]


def _select_bias(g, slope, i, nb):
    jidx = lax.broadcasted_iota(jnp.int32, g.shape, 0)
    own = jnp.zeros(g.shape, jnp.int32) + i
    rank = jnp.zeros(g.shape, F32)
    for kk in range(nb):
        gk = g[kk:kk + 1, :]
        beats = (gk > g) | ((gk == g) & (kk < jidx))
        rank = rank + jnp.where(beats & (own > kk), 1.0, 0.0)
    sel = (rank < MOBA_TOPK) & (jidx < own)
    rel = ((jidx - own) * MOBA_BLOCK).astype(F32) * slope
    return jnp.where(jidx == own, 0.0, jnp.where(sel, rel, NEG))


def _moba_schedule(items, nb):
    n, lag = len(items), MOBA_LAG
    steps = -(-(n + 2 * lag) // MOBA_UNROLL) * MOBA_UNROLL
    tab = np.zeros((5, steps), np.int32)
    for tau in range(steps):
        tab[1, tau], tab[0, tau] = items[tau] if tau < n else (nb - 1, 0)
        tab[2, tau] = items[tau - lag][0] if lag <= tau < n + lag else nb
        tab[4, tau], tab[3, tau] = items[tau - 2 * lag] if 2 * lag <= tau < n + 2 * lag else (nb, 0)
    return tab, steps


def _moba_kernel(own_ref, past_ref, qt_ref, k_ref, vt_ref, km_ref, al_ref, o_ref,
                 qs_ref, s_ref, p_ref, a_ref, m_ref, acc_ref, cb_ref, *, nb, own_steps, past_steps):
    tq = MOBA_BLOCK
    key_pos = lax.broadcasted_iota(jnp.int32, (MOBA_BLOCK, tq), 0)
    qry_pos = lax.broadcasted_iota(jnp.int32, (MOBA_BLOCK, tq), 1)
    cb_ref[...] = jnp.where(key_pos <= qry_pos, 0.0, NEG)
    for ref in (s_ref, p_ref, a_ref):
        ref[...] = jnp.zeros(ref.shape, ref.dtype)
    m_ref[nb] = jnp.zeros(m_ref.shape[1:], F32)
    acc_ref[nb] = jnp.zeros(acc_ref.shape[1:], F32)

    km = km_ref[...].astype(BF16)
    one_row = jnp.where(lax.broadcasted_iota(jnp.int32, (BF16_ROWS, tq), 0) == 0, 1.0, 0.0).astype(BF16)
    pad_rows = jnp.zeros((LANES - AUX_ONE - BF16_ROWS, tq), BF16)

    def gating(i, carry):
        qt = qt_ref[i]
        for e in range(MOBA_HEADS):
            qe = qt[e * HEAD_DIM:(e + 1) * HEAD_DIM]
            gate = _dot(km[:, e * LANES:e * LANES + HEAD_DIM], qe)
            slope = jnp.concatenate([al_ref[e:e + 1, :]] * (tq // LANES), axis=1)
            bias = _select_bias(gate, slope, i, nb)
            if nb < MAX_BLOCKS:
                bias = jnp.concatenate([bias, jnp.zeros((MAX_BLOCKS - nb, tq), F32)], axis=0)
            qs_ref[i, e] = jnp.concatenate([qe, bias.astype(BF16), one_row, pad_rows], axis=0)
        return carry

    lax.fori_loop(0, nb, gating, 0)

    def value_stage(tab_ref, tau, u, own):
        ring = u % MOBA_RING
        vblk = tab_ref[3, tau]
        slot = tab_ref[4, tau]
        for e in range(MOBA_HEADS):
            pv = _dot(vt_ref[vblk, e * V_ROWS:(e + 1) * V_ROWS, :], p_ref[ring, e])
            acc_ref[slot, e] = pv if own else a_ref[ring, e] * acc_ref[slot, e] + pv

    def softmax_stage(tab_ref, tau, u, own):
        ring = (u + MOBA_LAG) % MOBA_RING
        slot = tab_ref[2, tau]
        for e in range(MOBA_HEADS):
            s = s_ref[ring, e]
            m_new = jnp.max(s, axis=0, keepdims=True)
            if not own:
                m_prev = m_ref[slot, e]
                m_new = jnp.maximum(m_prev, m_new)
                a_ref[ring, e] = jnp.exp(m_prev - m_new)
            p_ref[ring, e] = jnp.exp(s - m_new).astype(BF16)
            m_ref[slot, e] = m_new

    def score_stage(tab_ref, tau, u, own):
        ring = u % MOBA_RING
        r0 = pl.multiple_of(tab_ref[0, tau] * MOBA_BLOCK, MOBA_BLOCK)
        qblk = tab_ref[1, tau]
        for e in range(MOBA_HEADS):
            s = _dot(k_ref[pl.ds(r0, MOBA_BLOCK), e * LANES:(e + 1) * LANES], qs_ref[qblk, e])
            s_ref[ring, e] = s + cb_ref[...] if own else s

    def pipeline(tab_ref, steps, own):
        def body(it, carry):
            for u in range(MOBA_UNROLL):
                for stage in (score_stage, value_stage, softmax_stage):
                    stage(tab_ref, MOBA_UNROLL * it + u, u, own)
            return carry

        lax.fori_loop(0, steps // MOBA_UNROLL, body, 0)

    pipeline(own_ref, own_steps, True)
    pipeline(past_ref, past_steps, False)

    def finalize(i, carry):
        outs = []
        for e in range(MOBA_HEADS):
            acc = acc_ref[i, e]
            outs.append(acc[0:HEAD_DIM] / acc[HEAD_DIM:HEAD_DIM + 1])
        q0 = pl.multiple_of(i * tq, tq)
        o_ref[pl.ds(q0, tq), :] = jnp.concatenate(outs, axis=0).T.astype(o_ref.dtype)
        return carry

    lax.fori_loop(0, nb, finalize, 0)


def _moba_prompt(qt, ka, va, km, batch, seq_len):
    nb = seq_len // MOBA_BLOCK
    assert nb <= MAX_BLOCKS and N_HEADS % MOBA_HEADS == 0
    groups = N_HEADS // MOBA_HEADS
    own_tab, own_steps = _moba_schedule([(i, i) for i in range(nb)], nb)
    past_tab, past_steps = _moba_schedule([(i, j) for i in range(nb) for j in range(i)], nb)
    slopes = jnp.broadcast_to(jnp.asarray(ALIBI_SLOPES, F32).reshape(groups, MOBA_HEADS, 1),
                              (groups, MOBA_HEADS, LANES))
    blk_spec = lambda r: pl.BlockSpec((None, nb, r, MOBA_BLOCK), lambda b, p, *_: (b, 0, p, 0))
    seq_spec = lambda w: pl.BlockSpec((None, seq_len, w), lambda b, p, *_: (b, 0, p))
    return pl.pallas_call(
        functools.partial(_moba_kernel, nb=nb, own_steps=own_steps, past_steps=past_steps),
        out_shape=jax.ShapeDtypeStruct((batch, seq_len, D_ATTN), BF16),
        grid_spec=pltpu.PrefetchScalarGridSpec(
            num_scalar_prefetch=2, grid=(batch, groups),
            in_specs=[blk_spec(MOBA_HEADS * HEAD_DIM), seq_spec(MOBA_HEADS * LANES),
                      blk_spec(MOBA_HEADS * V_ROWS),
                      pl.BlockSpec((None, nb, MOBA_HEADS * LANES), lambda b, p, *_: (b, 0, p)),
                      pl.BlockSpec((None, MOBA_HEADS, LANES), lambda b, p, *_: (p, 0, 0))],
            out_specs=seq_spec(MOBA_HEADS * HEAD_DIM),
            scratch_shapes=[pltpu.VMEM((nb, MOBA_HEADS, LANES, MOBA_BLOCK), BF16),
                            pltpu.VMEM((MOBA_RING, MOBA_HEADS, MOBA_BLOCK, MOBA_BLOCK), F32),
                            pltpu.VMEM((MOBA_RING, MOBA_HEADS, MOBA_BLOCK, MOBA_BLOCK), BF16),
                            pltpu.VMEM((MOBA_RING, MOBA_HEADS, 1, MOBA_BLOCK), F32),
                            pltpu.VMEM((nb + 1, MOBA_HEADS, 1, MOBA_BLOCK), F32),
                            pltpu.VMEM((nb + 1, MOBA_HEADS, V_ROWS, MOBA_BLOCK), F32),
                            pltpu.VMEM((MOBA_BLOCK, MOBA_BLOCK), F32)]),
        compiler_params=_params("parallel", "parallel"),
        name="moba_prompt",
    )(jnp.asarray(own_tab), jnp.asarray(past_tab), qt, ka.reshape(batch, seq_len, D_AUG), va,
      km.reshape(batch, nb, D_AUG), slopes)


def _head_rows(t):
    r = lax.broadcasted_iota(jnp.int32, (N_HEADS * t, 1), 0)
    return r // t, r % t


def _row_slopes(t):
    head, _ = _head_rows(t)
    slope = jnp.zeros((N_HEADS * t, 1), F32)
    for hh in range(N_HEADS):
        slope = jnp.where(head == hh, ALIBI_SLOPES[hh], slope)
    return slope


def _skeys_body(c, last, q_ref, kn_ref, pages, pp_ref, po_ref, inv_ref, qbd_ref, s_ref, g_ref,
                *, t, nblocks, past_len):
    rows = N_HEADS * t
    blocks_per_step = len(pages) // 2

    @pl.when(c == 0)
    def _():
        qt = jnp.concatenate([q_ref[...]] * N_HEADS, axis=0)
        head = lax.broadcasted_iota(jnp.int32, (rows, D_ATTN), 0) // t
        lane_head = lax.broadcasted_iota(jnp.int32, (rows, D_ATTN), 1) // HEAD_DIM
        qbd_ref[...] = jnp.where(head == lane_head, qt, 0.0).astype(BF16)
        g_ref[...] = jnp.zeros(g_ref.shape, F32)

    qbd = qbd_ref[...]
    g_lane = lax.broadcasted_iota(jnp.int32, (rows, LANES), 1)
    gate = g_ref[...]
    for bl in range(blocks_per_step):
        kt = jnp.concatenate([pages[2 * bl][...], pages[2 * bl + 1][...]], axis=1)
        j = c * blocks_per_step + bl
        s = _dot(qbd, kt.astype(BF16))
        s_ref[j] = s
        gate = jnp.where(g_lane == j, jnp.sum(s, axis=1, keepdims=True) * (1.0 / MOBA_BLOCK), gate)
    g_ref[...] = gate

    @pl.when(last)
    def _():
        rank = jnp.zeros(gate.shape, F32)
        for kk in range(nblocks):
            gk = gate[:, kk:kk + 1]
            rank = rank + jnp.where((gk > gate) | ((gk == gate) & (kk < g_lane)), 1.0, 0.0)
        unsel = jnp.where(rank < MOBA_TOPK, 0.0, NEG)
        slope = _row_slopes(t)
        _, qq = _head_rows(t)
        within = slope * lax.broadcasted_iota(jnp.int32, (rows, MOBA_BLOCK), 1).astype(F32)
        kn = jnp.concatenate([kn_ref[...], jnp.zeros((LANES - t, D_ATTN), F32)], axis=0).astype(BF16)
        r_own = lax.broadcasted_iota(jnp.int32, (rows, LANES), 1)
        s_own = jnp.where(r_own <= qq, _nt_dot(qbd, kn) + within[:, :LANES], NEG)
        m_run = jnp.full((rows, MOBA_BLOCK), NEG, F32)
        for j in range(nblocks):
            sj = s_ref[j] + within + (slope * float(j * MOBA_BLOCK - past_len) + unsel[:, j:j + 1])
            s_ref[j] = sj
            m_run = jnp.maximum(m_run, sj)
        m = jnp.maximum(jnp.max(m_run, axis=1, keepdims=True), jnp.max(s_own, axis=1, keepdims=True))
        p_own = jnp.exp(s_own - m)
        po_ref[...] = p_own.astype(po_ref.dtype)
        l_run = jnp.zeros((rows, MOBA_BLOCK), F32)
        for j in range(nblocks):
            pj = jnp.exp(s_ref[j] - m)
            pp_ref[j] = pj.astype(pp_ref.dtype)
            l_run = l_run + pj
        l = jnp.sum(l_run, axis=1, keepdims=True) + jnp.sum(p_own, axis=1, keepdims=True)
        inv_ref[...] = jnp.broadcast_to(1.0 / l, inv_ref.shape)


def _svals_body(c, last, pp_ref, po_ref, inv_ref, vn_ref, pages, o_ref, acc_ref, *, t):
    @pl.when(c == 0)
    def _():
        vn = jnp.concatenate([vn_ref[...], jnp.zeros((LANES - t, D_ATTN), F32)], axis=0).astype(BF16)
        acc_ref[...] = _dot(po_ref[...], vn)

    acc = acc_ref[...]
    for bl in range(len(pages) // 2):
        vt = jnp.concatenate([pages[2 * bl][...], pages[2 * bl + 1][...]], axis=1).astype(BF16)
        acc = acc + _nt_dot(pp_ref[bl], vt)
    acc_ref[...] = acc

    @pl.when(last)
    def _():
        scaled = acc * inv_ref[:, 0:1]
        lane_head = lax.broadcasted_iota(jnp.int32, (t, D_ATTN), 1) // HEAD_DIM
        out = jnp.zeros((t, D_ATTN), F32)
        for hh in range(N_HEADS):
            out = jnp.where(lane_head == hh, scaled[hh * t:(hh + 1) * t, :], out)
        o_ref[...] = out


def _merge_kernel(o_ref, sa_ref, u_ref, up_ref, sb_ref, ga_ref, gb_ref, x_ref,
                  wp_ref, ps_ref, wba_ref, wbp_ref, wo_ref, gp_ref, y_ref, ubuf_ref,
                  *, ns, tm, t0, zero_first_prev):
    i = pl.program_id(1)
    n = ns * tm
    prev = up_ref[...]
    if zero_first_prev:
        prev = jnp.where(i == 0, 0.0, prev)
    ubuf_ref[:, 0:PREV_ROWS, :] = prev
    ubuf_ref[:, PREV_ROWS:PREV_ROWS + tm, :] = u_ref[...]
    pos = t0 + i * tm + lax.broadcasted_iota(jnp.int32, (ns, tm, POOL_GROUP), 1)
    mixed = []
    for gidx, w in enumerate(POOL_WINDOWS):
        gl = slice(gidx * POOL_GROUP, (gidx + 1) * POOL_GROUP)
        cur = ubuf_ref[:, PREV_ROWS:PREV_ROWS + tm, gl]
        tot = cur
        for s in range(1, w):
            tot = tot + ubuf_ref[:, PREV_ROWS - s:PREV_ROWS - s + tm, gl]
        cnt = jnp.minimum(pos + 1, w).astype(F32)
        pooled = (tot / cnt - cur).reshape(n, POOL_GROUP)
        mixed.append(_dot(pooled.astype(BF16), wp_ref[gidx]))
    mixed = jnp.concatenate(mixed, axis=1) * ps_ref[...]

    def flat(ref):
        return ref[...].reshape(n, ref.shape[-1]).astype(F32)

    o_b = (mixed * flat(sb_ref)).astype(BF16)
    o_a = (flat(o_ref) * flat(sa_ref)).astype(BF16)
    m_a = _dot(o_a, wba_ref[...])
    m_b = _dot(o_b, wbp_ref[...])
    merged = flat(ga_ref) * m_a + flat(gb_ref) * m_b
    z = _dot(merged.astype(BF16), wo_ref[...])
    ms = jnp.mean(z * z, axis=-1, keepdims=True)
    y = flat(x_ref) + z * lax.rsqrt(ms + RMS_EPS) * gp_ref[...]
    y_ref[...] = y.reshape(y_ref.shape)


def _merge_vals_kernel(pt_ref, *refs, geo, merge_kw):
    del pt_ref
    n_in = 14
    merge_in, (pp_ref, po_ref, inv_ref, vn_ref) = refs[:n_in], refs[n_in:n_in + 4]
    pages = refs[n_in + 4:n_in + 4 + geo.pps]
    y_ref, os_ref, ubuf_ref, acc_ref = refs[n_in + 4 + geo.pps:]
    _merge_kernel(*merge_in, y_ref, ubuf_ref, **merge_kw)
    c = (pl.program_id(0) * pl.num_programs(1) + pl.program_id(1)) % geo.spp
    _svals_body(c, c == geo.spp - 1, pp_ref, po_ref, inv_ref, vn_ref, pages, os_ref, acc_ref, t=geo.t)


def _merge(o, sa, u, u_prev, sb, ga, gb, x, ws, *, ns, tm, t0, zero_first_prev, vals=None):
    nseq, seq_len, d_model = x.shape
    assert nseq % ns == 0 and seq_len % tm == 0 and tm % SUBLANES == 0
    assert ns == 1 or (tm == SUBLANES and all(a.dtype == F32 for a in (o, sa, sb, ga, gb)))
    tile = lambda width: pl.BlockSpec((ns, tm, width), lambda b, i, *_: (b, i, 0))
    full = lambda a: pl.BlockSpec(a.shape, lambda b, i, *_: (0,) * a.ndim, pipeline_mode=pl.Buffered(1))
    per = tm // PREV_ROWS if tm >= PREV_ROWS else 0
    prev_spec = pl.BlockSpec((ns, PREV_ROWS, D_POOL),
                             lambda b, i, *_: (b, jnp.maximum(i * per - 1, 0), 0))
    merge_kw = dict(ns=ns, tm=tm, t0=t0, zero_first_prev=zero_first_prev)
    in_specs = [tile(D_ATTN), tile(D_ATTN), tile(D_POOL), prev_spec, tile(D_POOL),
                tile(d_model), tile(d_model), tile(d_model)] + [full(w) for w in ws]
    ubuf = pltpu.VMEM((ns, PREV_ROWS + tm, D_POOL), F32)
    grid = (nseq // ns, seq_len // tm)
    if vals is None:
        return pl.pallas_call(
            functools.partial(_merge_kernel, **merge_kw), out_shape=jax.ShapeDtypeStruct(x.shape, F32),
            grid=grid, in_specs=in_specs, out_specs=tile(d_model), scratch_shapes=[ubuf],
            compiler_params=_params("parallel", "arbitrary"),
            name="merge_prompt" if ns == 1 else "merge_sample",
        )(o, sa, u, u_prev, sb, ga, gb, x, *ws)

    pp, po, inv, vn, cache_v, page_table, t = vals
    geo = _stream_geometry(page_table, cache_v, t, grid[0] * grid[1])
    step = lambda b, i: b * grid[1] + i
    seq_of = lambda b, i: step(b, i) // geo.spp
    new_spec = pl.BlockSpec((t, D_ATTN), lambda b, i, pt_ref: (seq_of(b, i), 0))
    per_seq = pl.BlockSpec((None, geo.rows, LANES), lambda b, i, pt_ref: (seq_of(b, i), 0, 0))
    page_spec = lambda n: pl.BlockSpec(
        (None, D_ATTN, LANES),
        lambda b, i, pt_ref: (pt_ref[seq_of(b, i) * geo.n_pages + (step(b, i) % geo.spp) * geo.pps + n], 0, 0))
    return pl.pallas_call(
        functools.partial(_merge_vals_kernel, geo=geo, merge_kw=merge_kw),
        out_shape=[jax.ShapeDtypeStruct(x.shape, F32), jax.ShapeDtypeStruct((geo.db * t, D_ATTN), F32)],
        grid_spec=pltpu.PrefetchScalarGridSpec(
            num_scalar_prefetch=1, grid=grid,
            in_specs=in_specs + [pl.BlockSpec((None, geo.pps // 2, geo.rows, MOBA_BLOCK),
                                              lambda b, i, pt_ref: (seq_of(b, i), step(b, i) % geo.spp, 0, 0)),
                                 per_seq, per_seq, new_spec] + [page_spec(n) for n in range(geo.pps)],
            out_specs=[tile(d_model), new_spec],
            scratch_shapes=[ubuf, pltpu.VMEM((geo.rows, D_ATTN), F32)]),
        compiler_params=_params("arbitrary", "arbitrary"), name="merge_prompt_values",
    )(page_table.reshape(-1).astype(jnp.int32), o, sa, u, u_prev, sb, ga, gb, x, *ws,
      pp, po, inv, vn, *([cache_v] * geo.pps))


def _layer(yp, ys, cache_k, cache_v, state, page_table, g_pre, w_in, w_pool, pool_scale,
           w_br_attn, w_br_pool, w_out, g_post):
    batch, seq_len, d_model = yp.shape
    db, t, _ = ys.shape
    past_len = page_table.shape[1] * cache_k.shape[2]

    bounds = [0]
    for width in (D_ATTN, D_ATTN, D_ATTN, D_ATTN, D_POOL, D_POOL, d_model, d_model):
        bounds.append(bounds[-1] + width)
    wq, wk, wv, wza, wu, wzb, wga, wgb = [w_in[:, a:b] for a, b in zip(bounds[:-1], bounds[1:])]
    proj_ws = [w.astype(BF16) for w in (wq * HEAD_DIM ** -0.5, wk, wv, wza, wu, wzb, wga, wgb)]
    g_pre2 = g_pre.reshape(1, d_model)
    merge_ws = [w_pool.astype(BF16), pool_scale.reshape(1, D_POOL), w_br_attn.astype(BF16),
                w_br_pool.astype(BF16), w_out.astype(BF16), g_post.reshape(1, d_model)]

    ks, vs, us, qs, sas, sbs, gas, gbs = _project(
        ys.reshape(db * t, d_model), g_pre2, proj_ws, prompt=False, seq_len=t)

    (kt, vt, up, qt, ka, va, sa, sb, ga, gb, km), (pp, po, inv) = _project(
        yp.reshape(batch * seq_len, d_model), g_pre2, proj_ws, prompt=True, seq_len=seq_len,
        keys=(qs, ks, cache_k, page_table, t))
    op = _moba_prompt(qt, ka, va, km, batch, seq_len)
    r3 = lambda a: a.reshape(batch, seq_len, a.shape[-1])
    up3 = r3(up)
    y_prompt, osamp = _merge(op, r3(sa), up3, up3, r3(sb), r3(ga), r3(gb), yp, merge_ws,
                             ns=1, tm=MOBA_BLOCK, t0=0, zero_first_prev=True,
                             vals=(pp, po, inv, vs, cache_v, page_table, t))
    seq_major = lambda a: jnp.transpose(a.reshape(batch, N_HEADS, HEAD_DIM, seq_len), (0, 3, 1, 2))

    s3 = lambda a: a.reshape(db, t, a.shape[-1])
    us3 = s3(us)
    state16 = jnp.concatenate([jnp.zeros((db, PREV_ROWS - POOL_STATE, D_POOL), F32), state], axis=1)
    y_sample = _merge(s3(osamp), s3(sas), us3, state16, s3(sbs), s3(gas), s3(gbs), ys, merge_ws,
                      ns=16, tm=t, t0=past_len, zero_first_prev=False)

    pool_prompt = up3[:, seq_len - POOL_STATE:]
    pool_sample = jnp.concatenate([state, us3], axis=1)[:, -POOL_STATE:]
    return (y_prompt, y_sample, seq_major(kt), seq_major(vt), pool_prompt,
            ks.reshape(db, t, N_HEADS, HEAD_DIM), vs.reshape(db, t, N_HEADS, HEAD_DIM), pool_sample)


def kernel(x_prompt, x_sample, cache_k, cache_v, state_pool, page_table, g_pre, w_in, w_pool,
           pool_scale, w_br_attn, w_br_pool, w_out, g_post):
    depth = w_in.shape[0]
    n_phys, page = cache_k.shape[1], cache_k.shape[2]
    pages_t = lambda c: jnp.transpose(c, (0, 2, 3, 1)).reshape(n_phys, D_ATTN, page)
    yp, ys = x_prompt, x_sample
    per_layer = []
    for l in range(depth):
        outs = _layer(yp, ys, pages_t(cache_k[l]), pages_t(cache_v[l]),
                      state_pool[l], page_table, g_pre[l], w_in[l], w_pool[l], pool_scale[l],
                      w_br_attn[l], w_br_pool[l], w_out[l], g_post[l])
        yp, ys = outs[0], outs[1]
        per_layer.append(outs[2:])
    stacked = [jnp.stack([layer[i] for layer in per_layer]) for i in range(6)]
    return (yp, ys, *stacked)
```

```python
import collections
import functools

import numpy as np
import jax
import jax.numpy as jnp
from jax import lax
from jax.experimental import pallas as pl
from jax.experimental.pallas import tpu as pltpu

F32 = jnp.float32
BF16 = jnp.bfloat16

N_HEADS = 8
HEAD_DIM = 64
D_ATTN = N_HEADS * HEAD_DIM
MOBA_BLOCK = 256
MOBA_TOPK = 3
POOL_WINDOWS = (2, 4, 8, 16)
POOL_GROUP = 128
D_POOL = POOL_GROUP * len(POOL_WINDOWS)
POOL_STATE = max(POOL_WINDOWS) - 1
PREV_ROWS = 16
RMS_EPS = 1e-6
LANES = 128
SUBLANES = 8
BF16_ROWS = 16
NEG = -1e30
ALIBI_SLOPES = tuple(2.0 ** (-(h + 1)) for h in range(N_HEADS))

AUX_BIAS = HEAD_DIM
MAX_BLOCKS = 16
AUX_ONE = AUX_BIAS + MAX_BLOCKS
D_AUG = N_HEADS * LANES
V_ROWS = HEAD_DIM + BF16_ROWS
D_VT = N_HEADS * V_ROWS

MOBA_HEADS = 4
MOBA_LAG = 2
MOBA_RING = 2 * MOBA_LAG
MOBA_UNROLL = MOBA_RING
VMEM_LIMIT = 56 * 1024 * 1024
PROJ_BLOCKS = 1


def _nt_dot(a, b):
    return lax.dot_general(a, b, (((1,), (1,)), ((), ())), preferred_element_type=F32)


def _dot(a, b):
    return jnp.dot(a, b, preferred_element_type=F32)


def _sigmoid(z):
    return 1.0 / (1.0 + jnp.exp(-z))


def _params(*semantics):
    return pltpu.CompilerParams(dimension_semantics=semantics, vmem_limit_bytes=VMEM_LIMIT)


def _proj_kernel(x_ref, g_ref, wq_ref, wk_ref, wv_ref, wza_ref, wu_ref, wzb_ref, wga_ref, wgb_ref,
                 *outs, prompt, blocks_per_seq):
    x = x_ref[...]
    ms = jnp.mean(x * x, axis=-1, keepdims=True)
    h = (x * lax.rsqrt(ms + RMS_EPS) * g_ref[...]).astype(BF16)

    def proj(w_ref):
        return _dot(h, w_ref[...])

    q = proj(wq_ref)
    k = proj(wk_ref)
    v = proj(wv_ref)
    za = proj(wza_ref)
    zb = proj(wzb_ref)
    if prompt:
        kt_ref, vt_ref, u_ref, qt_ref, ka_ref, va_ref, sa_ref, sb_ref, ga_ref, gb_ref, km_ref = outs
    else:
        k_ref, v_ref, u_ref, q_ref, sa_ref, sb_ref, ga_ref, gb_ref = outs
        q_ref[...] = q
        k_ref[...] = k
        v_ref[...] = v
    u_ref[...] = proj(wu_ref)
    sa_ref[...] = (za * _sigmoid(za)).astype(sa_ref.dtype)
    sb_ref[...] = (zb * _sigmoid(zb)).astype(sb_ref.dtype)
    ga_ref[...] = _sigmoid(proj(wga_ref)).astype(ga_ref.dtype)
    gb_ref[...] = _sigmoid(proj(wgb_ref)).astype(gb_ref.dtype)
    if not prompt:
        return

    v_t = v.T
    q_t = q.T.astype(BF16)
    kt_ref[...] = k.T
    vt_ref[...] = v_t
    ones_rows = jnp.where(lax.broadcasted_iota(jnp.int32, (BF16_ROWS, MOBA_BLOCK), 0) == 0, 1.0, 0.0)
    lane = lax.broadcasted_iota(jnp.int32, (MOBA_BLOCK, LANES), 1)
    kl = lax.broadcasted_iota(jnp.int32, (MOBA_BLOCK, LANES), 0).astype(F32)
    nblk = qt_ref.shape[0]
    for bl in range(nblk):
        rows = slice(bl * MOBA_BLOCK, (bl + 1) * MOBA_BLOCK)
        qt_ref[bl] = q_t[:, rows]
        pieces = []
        for hh in range(N_HEADS):
            pieces += [v_t[hh * HEAD_DIM:(hh + 1) * HEAD_DIM, rows], ones_rows]
        va_ref[bl] = jnp.concatenate(pieces, axis=0).astype(BF16)
        blk_in_seq = (pl.program_id(0) * nblk + bl) % blocks_per_seq
        onehot = jnp.where(lane - AUX_BIAS == blk_in_seq, 1.0, 0.0)
        for hh in range(N_HEADS):
            pair = k[rows, (hh // 2) * LANES:(hh // 2 + 1) * LANES]
            kd = pair if hh % 2 == 0 else pltpu.roll(pair, HEAD_DIM, 1)
            k_aux = jnp.where(lane == AUX_ONE, ALIBI_SLOPES[hh] * kl, onehot)
            dst = slice(hh * LANES, (hh + 1) * LANES)
            ka_ref[rows, dst] = jnp.where(lane < HEAD_DIM, kd, k_aux).astype(BF16)
            km_ref[bl, :, dst] = jnp.mean(kd, axis=0, keepdims=True)


StreamGeometry = collections.namedtuple(
    "StreamGeometry", "db n_pages past_len nblocks rows t spp pps")


def _stream_geometry(page_table, cache, t, host_steps):
    db, n_pages = page_table.shape
    page = cache.shape[2]
    past_len = n_pages * page
    assert page == LANES and 2 * page == MOBA_BLOCK and past_len % MOBA_BLOCK == 0 and t == SUBLANES
    assert host_steps % db == 0 and n_pages % (host_steps // db) == 0
    spp = host_steps // db
    pps = n_pages // spp
    nblocks = past_len // MOBA_BLOCK
    assert pps % 2 == 0 and nblocks <= LANES
    return StreamGeometry(db, n_pages, past_len, nblocks, N_HEADS * t, t, spp, pps)


def _proj_keys_kernel(pt_ref, *refs, n_proj_out, blocks_per_seq, geo):
    del pt_ref
    n_in = 10
    proj_in, (q_ref, kn_ref) = refs[:n_in], refs[n_in:n_in + 2]
    pages = refs[n_in + 2:n_in + 2 + geo.pps]
    outs = refs[n_in + 2 + geo.pps:]
    proj_out, (pp_ref, po_ref, inv_ref, qbd_ref, s_ref, g_ref) = outs[:n_proj_out], outs[n_proj_out:]
    _proj_kernel(*proj_in, *proj_out, prompt=True, blocks_per_seq=blocks_per_seq)
    c = pl.program_id(0) % geo.spp
    _skeys_body(c, c == geo.spp - 1, q_ref, kn_ref, pages, pp_ref, po_ref, inv_ref, qbd_ref, s_ref, g_ref,
                t=geo.t, nblocks=geo.nblocks, past_len=geo.past_len)


def _project(x2, g, ws, *, prompt, seq_len, keys=None):
    rows, d_model = x2.shape
    nblk = PROJ_BLOCKS if prompt else 1
    tm = nblk * MOBA_BLOCK
    assert rows % tm == 0
    inter = BF16 if prompt else F32
    row_spec = lambda width: pl.BlockSpec((tm, width), lambda i, *_: (i, 0))
    full = lambda a: pl.BlockSpec(a.shape, lambda i, *_: (0,) * a.ndim, pipeline_mode=pl.Buffered(1))
    sds = jax.ShapeDtypeStruct
    if prompt:
        assert seq_len % tm == 0 and seq_len // MOBA_BLOCK <= MAX_BLOCKS
        nb = seq_len // MOBA_BLOCK
        tps = seq_len // tm
        batch = rows // seq_len
        t_spec = pl.BlockSpec((None, D_ATTN, tm), lambda i, *_: (i // tps, 0, i % tps))
        blk_spec = lambda r: pl.BlockSpec((None, nblk, r, MOBA_BLOCK), lambda i, *_: (i // tps, i % tps, 0, 0))
        shapes = [sds((batch, D_ATTN, seq_len), F32), sds((batch, D_ATTN, seq_len), F32),
                  sds((rows, D_POOL), F32),
                  sds((batch, nb, D_ATTN, MOBA_BLOCK), BF16),
                  sds((rows, D_AUG), BF16),
                  sds((batch, nb, D_VT, MOBA_BLOCK), BF16)]
        specs = [t_spec, t_spec, row_spec(D_POOL), blk_spec(D_ATTN), row_spec(D_AUG), blk_spec(D_VT)]
    else:
        shapes = [sds((rows, D_ATTN), F32), sds((rows, D_ATTN), F32), sds((rows, D_POOL), F32),
                  sds((rows, D_ATTN), F32)]
        specs = [row_spec(D_ATTN), row_spec(D_ATTN), row_spec(D_POOL), row_spec(D_ATTN)]
    shapes += [sds((rows, D_ATTN), inter), sds((rows, D_POOL), inter),
               sds((rows, d_model), inter), sds((rows, d_model), inter)]
    specs += [row_spec(D_ATTN), row_spec(D_POOL), row_spec(d_model), row_spec(d_model)]
    if prompt:
        shapes += [sds((rows // MOBA_BLOCK, 1, D_AUG), F32)]
        specs += [pl.BlockSpec((nblk, 1, D_AUG), lambda i, *_: (i, 0, 0))]
    blocks_per_seq = (seq_len // MOBA_BLOCK) if prompt else 1
    in_specs = [row_spec(d_model), full(g)] + [full(w) for w in ws]
    if keys is None:
        return pl.pallas_call(
            functools.partial(_proj_kernel, prompt=prompt, blocks_per_seq=blocks_per_seq),
            out_shape=shapes, grid=(rows // tm,), in_specs=in_specs, out_specs=specs,
            compiler_params=_params("parallel"), name="proj_prompt" if prompt else "proj_sample",
        )(x2, g, *ws)

    q_s, kn_s, cache_k, page_table, t = keys
    geo = _stream_geometry(page_table, cache_k, t, rows // tm)
    seq_of = lambda i: i // geo.spp
    new_spec = pl.BlockSpec((t, D_ATTN), lambda i, pt_ref: (seq_of(i), 0))
    per_seq = pl.BlockSpec((None, geo.rows, LANES), lambda i, pt_ref: (seq_of(i), 0, 0))
    page_spec = lambda n: pl.BlockSpec(
        (None, D_ATTN, LANES),
        lambda i, pt_ref: (pt_ref[seq_of(i) * geo.n_pages + (i % geo.spp) * geo.pps + n], 0, 0))
    outs = pl.pallas_call(
        functools.partial(_proj_keys_kernel, n_proj_out=len(shapes), blocks_per_seq=blocks_per_seq, geo=geo),
        out_shape=shapes + [sds((geo.db, geo.nblocks, geo.rows, MOBA_BLOCK), BF16),
                            sds((geo.db, geo.rows, LANES), BF16), sds((geo.db, geo.rows, LANES), F32)],
        grid_spec=pltpu.PrefetchScalarGridSpec(
            num_scalar_prefetch=1, grid=(rows // tm,),
            in_specs=in_specs + [new_spec, new_spec] + [page_spec(n) for n in range(geo.pps)],
            out_specs=specs + [pl.BlockSpec((None, geo.nblocks, geo.rows, MOBA_BLOCK),
                                            lambda i, pt_ref: (seq_of(i), 0, 0, 0)), per_seq, per_seq],
            scratch_shapes=[pltpu.VMEM((geo.rows, D_ATTN), BF16),
                            pltpu.VMEM((geo.nblocks, geo.rows, MOBA_BLOCK), F32),
                            pltpu.VMEM((geo.rows, LANES), F32)]),
        compiler_params=_params("arbitrary"), name="proj_prompt_keys",
    )(page_table.reshape(-1).astype(jnp.int32), x2, g, *ws, q_s, kn_s, *([cache_k] * geo.pps))
    return outs[:len(shapes)], outs[len(shapes):]


def _select_bias(g, slope, i, nb):
    jidx = lax.broadcasted_iota(jnp.int32, g.shape, 0)
    own = jnp.zeros(g.shape, jnp.int32) + i
    work = jnp.where(jidx < own, g, -jnp.inf)
    sel = jidx == own
    for _ in range(MOBA_TOPK):
        best = jnp.max(work, axis=0, keepdims=True)
        at_best = (work == best) & (best > -jnp.inf)
        first = jnp.min(jnp.where(at_best, jidx, nb), axis=0, keepdims=True)
        pick = jidx == first
        sel = sel | pick
        work = jnp.where(pick, -jnp.inf, work)
    rel = ((jidx - own) * MOBA_BLOCK).astype(F32) * slope
    return jnp.where(sel, rel, NEG)


def _moba_schedule(items, nb):
    n, lag = len(items), MOBA_LAG
    steps = -(-(n + 2 * lag) // MOBA_UNROLL) * MOBA_UNROLL
    tab = np.zeros((5, steps), np.int32)
    for tau in range(steps):
        tab[1, tau], tab[0, tau] = items[tau] if tau < n else (nb - 1, 0)
        tab[2, tau] = items[tau - lag][0] if lag <= tau < n + lag else nb
        tab[4, tau], tab[3, tau] = items[tau - 2 * lag] if 2 * lag <= tau < n + 2 * lag else (nb, 0)
    return tab, steps


def _moba_kernel(own_ref, past_ref, qt_ref, k_ref, vt_ref, km_ref, al_ref, o_ref,
                 qs_ref, s_ref, p_ref, a_ref, x_ref, m_ref, acc_ref, cb_ref, *, nb, own_steps, past_steps):
    tq = MOBA_BLOCK
    key_pos = lax.broadcasted_iota(jnp.int32, (MOBA_BLOCK, tq), 0)
    qry_pos = lax.broadcasted_iota(jnp.int32, (MOBA_BLOCK, tq), 1)
    cb_ref[...] = jnp.where(key_pos <= qry_pos, 0.0, NEG)
    for ref in (s_ref, p_ref, a_ref, x_ref):
        ref[...] = jnp.zeros(ref.shape, ref.dtype)
    m_ref[nb] = jnp.zeros(m_ref.shape[1:], F32)
    acc_ref[nb] = jnp.zeros(acc_ref.shape[1:], F32)

    km = km_ref[...].astype(BF16)
    one_row = jnp.where(lax.broadcasted_iota(jnp.int32, (BF16_ROWS, tq), 0) == 0, 1.0, 0.0).astype(BF16)
    pad_rows = jnp.zeros((LANES - AUX_ONE - BF16_ROWS, tq), BF16)

    def gating(i, carry):
        qt = qt_ref[i]
        for e in range(MOBA_HEADS):
            qe = qt[e * HEAD_DIM:(e + 1) * HEAD_DIM]
            gate = _dot(km[:, e * LANES:e * LANES + HEAD_DIM], qe)
            slope = jnp.concatenate([al_ref[e:e + 1, :]] * (tq // LANES), axis=1)
            bias = _select_bias(gate, slope, i, nb)
            if nb < MAX_BLOCKS:
                bias = jnp.concatenate([bias, jnp.zeros((MAX_BLOCKS - nb, tq), F32)], axis=0)
            qs_ref[i, e] = jnp.concatenate([qe, bias.astype(BF16), one_row, pad_rows], axis=0)
        return carry

    lax.fori_loop(0, nb, gating, 0)

    def value_stage(tab_ref, tau, u, own):
        ring = u % MOBA_RING
        vblk = tab_ref[3, tau]
        slot = tab_ref[4, tau]
        for e in range(MOBA_HEADS):
            pv = _dot(vt_ref[vblk, e * V_ROWS:(e + 1) * V_ROWS, :], p_ref[ring, e])
            acc_ref[slot, e] = pv if own else a_ref[ring, e] * acc_ref[slot, e] + pv

    def softmax_stage(tab_ref, tau, u, own):
        ring = (u + MOBA_LAG) % MOBA_RING
        slot = tab_ref[2, tau]
        for e in range(MOBA_HEADS):
            m_new = x_ref[ring, e]
            if not own:
                m_prev = m_ref[slot, e]
                m_new = jnp.maximum(m_prev, m_new)
                a_ref[ring, e] = jnp.exp(m_prev - m_new)
            p_ref[ring, e] = jnp.exp((s_ref[ring, e] - m_new).astype(BF16))
            m_ref[slot, e] = m_new

    def score_stage(tab_ref, tau, u, own):
        ring = u % MOBA_RING
        r0 = pl.multiple_of(tab_ref[0, tau] * MOBA_BLOCK, MOBA_BLOCK)
        qblk = tab_ref[1, tau]
        for e in range(MOBA_HEADS):
            s = _dot(k_ref[pl.ds(r0, MOBA_BLOCK), e * LANES:(e + 1) * LANES], qs_ref[qblk, e])
            if own:
                s = s + cb_ref[...]
            s_ref[ring, e] = s
            x_ref[ring, e] = jnp.max(s, axis=0, keepdims=True)

    def pipeline(tab_ref, steps, own):
        def body(it, carry):
            for u in range(MOBA_UNROLL):
                for stage in (score_stage, value_stage, softmax_stage):
                    stage(tab_ref, MOBA_UNROLL * it + u, u, own)
            return carry

        lax.fori_loop(0, steps // MOBA_UNROLL, body, 0)

    pipeline(own_ref, own_steps, True)
    pipeline(past_ref, past_steps, False)

    def finalize(i, carry):
        outs = []
        for e in range(MOBA_HEADS):
            acc = acc_ref[i, e]
            outs.append(acc[0:HEAD_DIM] / acc[HEAD_DIM:HEAD_DIM + 1])
        q0 = pl.multiple_of(i * tq, tq)
        o_ref[pl.ds(q0, tq), :] = jnp.concatenate(outs, axis=0).T.astype(o_ref.dtype)
        return carry

    lax.fori_loop(0, nb, finalize, 0)


def _moba_prompt(qt, ka, va, km, batch, seq_len):
    nb = seq_len // MOBA_BLOCK
    assert nb <= MAX_BLOCKS and N_HEADS % MOBA_HEADS == 0
    groups = N_HEADS // MOBA_HEADS
    own_tab, own_steps = _moba_schedule([(i, i) for i in range(nb)], nb)
    past_tab, past_steps = _moba_schedule([(i, j) for i in range(nb) for j in range(i)], nb)
    slopes = jnp.broadcast_to(jnp.asarray(ALIBI_SLOPES, F32).reshape(groups, MOBA_HEADS, 1),
                              (groups, MOBA_HEADS, LANES))
    blk_spec = lambda r: pl.BlockSpec((None, nb, r, MOBA_BLOCK), lambda b, p, *_: (b, 0, p, 0))
    seq_spec = lambda w: pl.BlockSpec((None, seq_len, w), lambda b, p, *_: (b, 0, p))
    return pl.pallas_call(
        functools.partial(_moba_kernel, nb=nb, own_steps=own_steps, past_steps=past_steps),
        out_shape=jax.ShapeDtypeStruct((batch, seq_len, D_ATTN), BF16),
        grid_spec=pltpu.PrefetchScalarGridSpec(
            num_scalar_prefetch=2, grid=(batch, groups),
            in_specs=[blk_spec(MOBA_HEADS * HEAD_DIM), seq_spec(MOBA_HEADS * LANES),
                      blk_spec(MOBA_HEADS * V_ROWS),
                      pl.BlockSpec((None, nb, MOBA_HEADS * LANES), lambda b, p, *_: (b, 0, p)),
                      pl.BlockSpec((None, MOBA_HEADS, LANES), lambda b, p, *_: (p, 0, 0))],
            out_specs=seq_spec(MOBA_HEADS * HEAD_DIM),
            scratch_shapes=[pltpu.VMEM((nb, MOBA_HEADS, LANES, MOBA_BLOCK), BF16),
                            pltpu.VMEM((MOBA_RING, MOBA_HEADS, MOBA_BLOCK, MOBA_BLOCK), F32),
                            pltpu.VMEM((MOBA_RING, MOBA_HEADS, MOBA_BLOCK, MOBA_BLOCK), BF16),
                            pltpu.VMEM((MOBA_RING, MOBA_HEADS, 1, MOBA_BLOCK), F32),
                            pltpu.VMEM((MOBA_RING, MOBA_HEADS, 1, MOBA_BLOCK), F32),
                            pltpu.VMEM((nb + 1, MOBA_HEADS, 1, MOBA_BLOCK), F32),
                            pltpu.VMEM((nb + 1, MOBA_HEADS, V_ROWS, MOBA_BLOCK), F32),
                            pltpu.VMEM((MOBA_BLOCK, MOBA_BLOCK), F32)]),
        compiler_params=_params("parallel", "parallel"),
        name="moba_prompt",
    )(jnp.asarray(own_tab), jnp.asarray(past_tab), qt, ka.reshape(batch, seq_len, D_AUG), va,
      km.reshape(batch, nb, D_AUG), slopes)


def _head_rows(t):
    r = lax.broadcasted_iota(jnp.int32, (N_HEADS * t, 1), 0)
    return r // t, r % t


def _row_slopes(t):
    head, _ = _head_rows(t)
    slope = jnp.zeros((N_HEADS * t, 1), F32)
    for hh in range(N_HEADS):
        slope = jnp.where(head == hh, ALIBI_SLOPES[hh], slope)
    return slope


def _skeys_body(c, last, q_ref, kn_ref, pages, pp_ref, po_ref, inv_ref, qbd_ref, s_ref, g_ref,
                *, t, nblocks, past_len):
    rows = N_HEADS * t
    blocks_per_step = len(pages) // 2

    @pl.when(c == 0)
    def _():
        qt = jnp.concatenate([q_ref[...]] * N_HEADS, axis=0)
        head = lax.broadcasted_iota(jnp.int32, (rows, D_ATTN), 0) // t
        lane_head = lax.broadcasted_iota(jnp.int32, (rows, D_ATTN), 1) // HEAD_DIM
        qbd_ref[...] = jnp.where(head == lane_head, qt, 0.0).astype(BF16)
        g_ref[...] = jnp.zeros(g_ref.shape, F32)

    qbd = qbd_ref[...]
    g_lane = lax.broadcasted_iota(jnp.int32, (rows, LANES), 1)
    gate = g_ref[...]
    for bl in range(blocks_per_step):
        kt = jnp.concatenate([pages[2 * bl][...], pages[2 * bl + 1][...]], axis=1)
        j = c * blocks_per_step + bl
        s = _dot(qbd, kt.astype(BF16))
        s_ref[j] = s
        gate = jnp.where(g_lane == j, jnp.sum(s, axis=1, keepdims=True) * (1.0 / MOBA_BLOCK), gate)
    g_ref[...] = gate

    @pl.when(last)
    def _():
        rank = jnp.zeros(gate.shape, F32)
        for kk in range(nblocks):
            gk = gate[:, kk:kk + 1]
            rank = rank + jnp.where((gk > gate) | ((gk == gate) & (kk < g_lane)), 1.0, 0.0)
        unsel = jnp.where(rank < MOBA_TOPK, 0.0, NEG)
        slope = _row_slopes(t)
        _, qq = _head_rows(t)
        within = slope * lax.broadcasted_iota(jnp.int32, (rows, MOBA_BLOCK), 1).astype(F32)
        kn = jnp.concatenate([kn_ref[...], jnp.zeros((LANES - t, D_ATTN), F32)], axis=0).astype(BF16)
        r_own = lax.broadcasted_iota(jnp.int32, (rows, LANES), 1)
        s_own = jnp.where(r_own <= qq, _nt_dot(qbd, kn) + within[:, :LANES], NEG)
        m_run = jnp.full((rows, MOBA_BLOCK), NEG, F32)
        for j in range(nblocks):
            sj = s_ref[j] + within + (slope * float(j * MOBA_BLOCK - past_len) + unsel[:, j:j + 1])
            s_ref[j] = sj
            m_run = jnp.maximum(m_run, sj)
        m = jnp.maximum(jnp.max(m_run, axis=1, keepdims=True), jnp.max(s_own, axis=1, keepdims=True))
        p_own = jnp.exp(s_own - m)
        po_ref[...] = p_own.astype(po_ref.dtype)
        l_run = jnp.zeros((rows, MOBA_BLOCK), F32)
        for j in range(nblocks):
            pj = jnp.exp(s_ref[j] - m)
            pp_ref[j] = pj.astype(pp_ref.dtype)
            l_run = l_run + pj
        l = jnp.sum(l_run, axis=1, keepdims=True) + jnp.sum(p_own, axis=1, keepdims=True)
        inv_ref[...] = jnp.broadcast_to(1.0 / l, inv_ref.shape)


def _svals_body(c, last, pp_ref, po_ref, inv_ref, vn_ref, pages, o_ref, acc_ref, *, t):
    @pl.when(c == 0)
    def _():
        vn = jnp.concatenate([vn_ref[...], jnp.zeros((LANES - t, D_ATTN), F32)], axis=0).astype(BF16)
        acc_ref[...] = _dot(po_ref[...], vn)

    acc = acc_ref[...]
    for bl in range(len(pages) // 2):
        vt = jnp.concatenate([pages[2 * bl][...], pages[2 * bl + 1][...]], axis=1).astype(BF16)
        acc = acc + _nt_dot(pp_ref[bl], vt)
    acc_ref[...] = acc

    @pl.when(last)
    def _():
        scaled = acc * inv_ref[:, 0:1]
        lane_head = lax.broadcasted_iota(jnp.int32, (t, D_ATTN), 1) // HEAD_DIM
        out = jnp.zeros((t, D_ATTN), F32)
        for hh in range(N_HEADS):
            out = jnp.where(lane_head == hh, scaled[hh * t:(hh + 1) * t, :], out)
        o_ref[...] = out


def _merge_kernel(o_ref, sa_ref, u_ref, up_ref, sb_ref, ga_ref, gb_ref, x_ref,
                  wp_ref, ps_ref, wba_ref, wbp_ref, wo_ref, gp_ref, y_ref, ubuf_ref,
                  *, ns, tm, t0, zero_first_prev):
    i = pl.program_id(1)
    n = ns * tm
    prev = up_ref[...]
    if zero_first_prev:
        prev = jnp.where(i == 0, 0.0, prev)
    ubuf_ref[:, 0:PREV_ROWS, :] = prev
    ubuf_ref[:, PREV_ROWS:PREV_ROWS + tm, :] = u_ref[...]
    pos = t0 + i * tm + lax.broadcasted_iota(jnp.int32, (ns, tm, POOL_GROUP), 1)
    mixed = []
    for gidx, w in enumerate(POOL_WINDOWS):
        gl = slice(gidx * POOL_GROUP, (gidx + 1) * POOL_GROUP)
        cur = ubuf_ref[:, PREV_ROWS:PREV_ROWS + tm, gl]
        tot = cur
        for s in range(1, w):
            tot = tot + ubuf_ref[:, PREV_ROWS - s:PREV_ROWS - s + tm, gl]
        cnt = jnp.minimum(pos + 1, w).astype(F32)
        pooled = (tot / cnt - cur).reshape(n, POOL_GROUP)
        mixed.append(_dot(pooled.astype(BF16), wp_ref[gidx]))
    mixed = jnp.concatenate(mixed, axis=1) * ps_ref[...]

    def flat(ref):
        return ref[...].reshape(n, ref.shape[-1]).astype(F32)

    o_b = (mixed * flat(sb_ref)).astype(BF16)
    o_a = (flat(o_ref) * flat(sa_ref)).astype(BF16)
    m_a = _dot(o_a, wba_ref[...])
    m_b = _dot(o_b, wbp_ref[...])
    merged = flat(ga_ref) * m_a + flat(gb_ref) * m_b
    z = _dot(merged.astype(BF16), wo_ref[...])
    ms = jnp.mean(z * z, axis=-1, keepdims=True)
    y = flat(x_ref) + z * lax.rsqrt(ms + RMS_EPS) * gp_ref[...]
    y_ref[...] = y.reshape(y_ref.shape)


def _merge_vals_kernel(pt_ref, *refs, geo, merge_kw):
    del pt_ref
    n_in = 14
    merge_in, (pp_ref, po_ref, inv_ref, vn_ref) = refs[:n_in], refs[n_in:n_in + 4]
    pages = refs[n_in + 4:n_in + 4 + geo.pps]
    y_ref, os_ref, ubuf_ref, acc_ref = refs[n_in + 4 + geo.pps:]
    _merge_kernel(*merge_in, y_ref, ubuf_ref, **merge_kw)
    c = (pl.program_id(0) * pl.num_programs(1) + pl.program_id(1)) % geo.spp
    _svals_body(c, c == geo.spp - 1, pp_ref, po_ref, inv_ref, vn_ref, pages, os_ref, acc_ref, t=geo.t)


def _merge(o, sa, u, u_prev, sb, ga, gb, x, ws, *, ns, tm, t0, zero_first_prev, vals=None):
    nseq, seq_len, d_model = x.shape
    assert nseq % ns == 0 and seq_len % tm == 0 and tm % SUBLANES == 0
    assert ns == 1 or (tm == SUBLANES and all(a.dtype == F32 for a in (o, sa, sb, ga, gb)))
    tile = lambda width: pl.BlockSpec((ns, tm, width), lambda b, i, *_: (b, i, 0))
    full = lambda a: pl.BlockSpec(a.shape, lambda b, i, *_: (0,) * a.ndim, pipeline_mode=pl.Buffered(1))
    per = tm // PREV_ROWS if tm >= PREV_ROWS else 0
    prev_spec = pl.BlockSpec((ns, PREV_ROWS, D_POOL),
                             lambda b, i, *_: (b, jnp.maximum(i * per - 1, 0), 0))
    merge_kw = dict(ns=ns, tm=tm, t0=t0, zero_first_prev=zero_first_prev)
    in_specs = [tile(D_ATTN), tile(D_ATTN), tile(D_POOL), prev_spec, tile(D_POOL),
                tile(d_model), tile(d_model), tile(d_model)] + [full(w) for w in ws]
    ubuf = pltpu.VMEM((ns, PREV_ROWS + tm, D_POOL), F32)
    grid = (nseq // ns, seq_len // tm)
    if vals is None:
        return pl.pallas_call(
            functools.partial(_merge_kernel, **merge_kw), out_shape=jax.ShapeDtypeStruct(x.shape, F32),
            grid=grid, in_specs=in_specs, out_specs=tile(d_model), scratch_shapes=[ubuf],
            compiler_params=_params("parallel", "arbitrary"),
            name="merge_prompt" if ns == 1 else "merge_sample",
        )(o, sa, u, u_prev, sb, ga, gb, x, *ws)

    pp, po, inv, vn, cache_v, page_table, t = vals
    geo = _stream_geometry(page_table, cache_v, t, grid[0] * grid[1])
    step = lambda b, i: b * grid[1] + i
    seq_of = lambda b, i: step(b, i) // geo.spp
    new_spec = pl.BlockSpec((t, D_ATTN), lambda b, i, pt_ref: (seq_of(b, i), 0))
    per_seq = pl.BlockSpec((None, geo.rows, LANES), lambda b, i, pt_ref: (seq_of(b, i), 0, 0))
    page_spec = lambda n: pl.BlockSpec(
        (None, D_ATTN, LANES),
        lambda b, i, pt_ref: (pt_ref[seq_of(b, i) * geo.n_pages + (step(b, i) % geo.spp) * geo.pps + n], 0, 0))
    return pl.pallas_call(
        functools.partial(_merge_vals_kernel, geo=geo, merge_kw=merge_kw),
        out_shape=[jax.ShapeDtypeStruct(x.shape, F32), jax.ShapeDtypeStruct((geo.db * t, D_ATTN), F32)],
        grid_spec=pltpu.PrefetchScalarGridSpec(
            num_scalar_prefetch=1, grid=grid,
            in_specs=in_specs + [pl.BlockSpec((None, geo.pps // 2, geo.rows, MOBA_BLOCK),
                                              lambda b, i, pt_ref: (seq_of(b, i), step(b, i) % geo.spp, 0, 0)),
                                 per_seq, per_seq, new_spec] + [page_spec(n) for n in range(geo.pps)],
            out_specs=[tile(d_model), new_spec],
            scratch_shapes=[ubuf, pltpu.VMEM((geo.rows, D_ATTN), F32)]),
        compiler_params=_params("arbitrary", "arbitrary"), name="merge_prompt_values",
    )(page_table.reshape(-1).astype(jnp.int32), o, sa, u, u_prev, sb, ga, gb, x, *ws,
      pp, po, inv, vn, *([cache_v] * geo.pps))


def _layer(yp, ys, cache_k, cache_v, state, page_table, g_pre, w_in, w_pool, pool_scale,
           w_br_attn, w_br_pool, w_out, g_post):
    batch, seq_len, d_model = yp.shape
    db, t, _ = ys.shape
    past_len = page_table.shape[1] * cache_k.shape[2]

    bounds = [0]
    for width in (D_ATTN, D_ATTN, D_ATTN, D_ATTN, D_POOL, D_POOL, d_model, d_model):
        bounds.append(bounds[-1] + width)
    wq, wk, wv, wza, wu, wzb, wga, wgb = [w_in[:, a:b] for a, b in zip(bounds[:-1], bounds[1:])]
    proj_ws = [w.astype(BF16) for w in (wq * HEAD_DIM ** -0.5, wk, wv, wza, wu, wzb, wga, wgb)]
    g_pre2 = g_pre.reshape(1, d_model)
    merge_ws = [w_pool.astype(BF16), pool_scale.reshape(1, D_POOL), w_br_attn.astype(BF16),
                w_br_pool.astype(BF16), w_out.astype(BF16), g_post.reshape(1, d_model)]

    ks, vs, us, qs, sas, sbs, gas, gbs = _project(
        ys.reshape(db * t, d_model), g_pre2, proj_ws, prompt=False, seq_len=t)

    (kt, vt, up, qt, ka, va, sa, sb, ga, gb, km), (pp, po, inv) = _project(
        yp.reshape(batch * seq_len, d_model), g_pre2, proj_ws, prompt=True, seq_len=seq_len,
        keys=(qs, ks, cache_k, page_table, t))
    op = _moba_prompt(qt, ka, va, km, batch, seq_len)
    r3 = lambda a: a.reshape(batch, seq_len, a.shape[-1])
    up3 = r3(up)
    y_prompt, osamp = _merge(op, r3(sa), up3, up3, r3(sb), r3(ga), r3(gb), yp, merge_ws,
                             ns=1, tm=MOBA_BLOCK, t0=0, zero_first_prev=True,
                             vals=(pp, po, inv, vs, cache_v, page_table, t))
    seq_major = lambda a: jnp.transpose(a.reshape(batch, N_HEADS, HEAD_DIM, seq_len), (0, 3, 1, 2))

    s3 = lambda a: a.reshape(db, t, a.shape[-1])
    us3 = s3(us)
    state16 = jnp.concatenate([jnp.zeros((db, PREV_ROWS - POOL_STATE, D_POOL), F32), state], axis=1)
    y_sample = _merge(s3(osamp), s3(sas), us3, state16, s3(sbs), s3(gas), s3(gbs), ys, merge_ws,
                      ns=16, tm=t, t0=past_len, zero_first_prev=False)

    pool_prompt = up3[:, seq_len - POOL_STATE:]
    pool_sample = jnp.concatenate([state, us3], axis=1)[:, -POOL_STATE:]
    return (y_prompt, y_sample, seq_major(kt), seq_major(vt), pool_prompt,
            ks.reshape(db, t, N_HEADS, HEAD_DIM), vs.reshape(db, t, N_HEADS, HEAD_DIM), pool_sample)


def kernel(x_prompt, x_sample, cache_k, cache_v, state_pool, page_table, g_pre, w_in, w_pool,
           pool_scale, w_br_attn, w_br_pool, w_out, g_post):
    depth = w_in.shape[0]
    n_phys, page = cache_k.shape[1], cache_k.shape[2]
    pages_t = lambda c: jnp.transpose(c, (0, 2, 3, 1)).reshape(n_phys, D_ATTN, page)
    yp, ys = x_prompt, x_sample
    per_layer = []
    for l in range(depth):
        outs = _layer(yp, ys, pages_t(cache_k[l]), pages_t(cache_v[l]),
                      state_pool[l], page_table, g_pre[l], w_in[l], w_pool[l], pool_scale[l],
                      w_br_attn[l], w_br_pool[l], w_out[l], g_post[l])
        yp, ys = outs[0], outs[1]
        per_layer.append(outs[2:])
    stacked = [jnp.stack([layer[i] for layer in per_layer]) for i in range(6)]
    return (yp, ys, *stacked)
```

```python
import collections
import functools

import numpy as np
import jax
import jax.numpy as jnp
from jax import lax
from jax.experimental import pallas as pl
from jax.experimental.pallas import tpu as pltpu

F32 = jnp.float32
BF16 = jnp.bfloat16

N_HEADS = 8
HEAD_DIM = 64
D_ATTN = N_HEADS * HEAD_DIM
MOBA_BLOCK = 256
MOBA_TOPK = 3
POOL_WINDOWS = (2, 4, 8, 16)
POOL_GROUP = 128
D_POOL = POOL_GROUP * len(POOL_WINDOWS)
POOL_STATE = max(POOL_WINDOWS) - 1
PREV_ROWS = 16
RMS_EPS = 1e-6
LANES = 128
SUBLANES = 8
BF16_ROWS = 16
NEG = -1e30
ALIBI_SLOPES = tuple(2.0 ** (-(h + 1)) for h in range(N_HEADS))

AUX_BIAS = HEAD_DIM
MAX_BLOCKS = 16
AUX_ONE = AUX_BIAS + MAX_BLOCKS
D_AUG = N_HEADS * LANES
V_ROWS = HEAD_DIM + BF16_ROWS
D_VT = N_HEADS * V_ROWS

MOBA_HEADS = 4
MOBA_LAG = 2
MOBA_RING = 2 * MOBA_LAG
MOBA_UNROLL = MOBA_RING
VMEM_LIMIT = 56 * 1024 * 1024
PROJ_BLOCKS = 1


def _nt_dot(a, b):
    return lax.dot_general(a, b, (((1,), (1,)), ((), ())), preferred_element_type=F32)


def _dot(a, b):
    return jnp.dot(a, b, preferred_element_type=F32)


def _sigmoid(z):
    return 1.0 / (1.0 + jnp.exp(-z))


def _params(*semantics):
    return pltpu.CompilerParams(dimension_semantics=semantics, vmem_limit_bytes=VMEM_LIMIT)


def _proj_kernel(x_ref, g_ref, wq_ref, wk_ref, wv_ref, wza_ref, wu_ref, wzb_ref, wga_ref, wgb_ref,
                 *outs, prompt, blocks_per_seq):
    x = x_ref[...]
    ms = jnp.mean(x * x, axis=-1, keepdims=True)
    h = (x * lax.rsqrt(ms + RMS_EPS) * g_ref[...]).astype(BF16)

    def proj(w_ref):
        return _dot(h, w_ref[...])

    q = proj(wq_ref)
    k = proj(wk_ref)
    v = proj(wv_ref)
    za = proj(wza_ref)
    zb = proj(wzb_ref)
    if prompt:
        kt_ref, vt_ref, u_ref, qt_ref, ka_ref, va_ref, sa_ref, sb_ref, ga_ref, gb_ref, km_ref = outs
    else:
        k_ref, v_ref, u_ref, q_ref, sa_ref, sb_ref, ga_ref, gb_ref = outs
        q_ref[...] = q
        k_ref[...] = k
        v_ref[...] = v
    u_ref[...] = proj(wu_ref)
    sa_ref[...] = (za * _sigmoid(za)).astype(sa_ref.dtype)
    sb_ref[...] = (zb * _sigmoid(zb)).astype(sb_ref.dtype)
    ga_ref[...] = _sigmoid(proj(wga_ref)).astype(ga_ref.dtype)
    gb_ref[...] = _sigmoid(proj(wgb_ref)).astype(gb_ref.dtype)
    if not prompt:
        return

    v_t = v.T
    q_t = q.T.astype(BF16)
    kt_ref[...] = k.T
    vt_ref[...] = v_t
    ones_rows = jnp.where(lax.broadcasted_iota(jnp.int32, (BF16_ROWS, MOBA_BLOCK), 0) == 0, 1.0, 0.0)
    lane = lax.broadcasted_iota(jnp.int32, (MOBA_BLOCK, LANES), 1)
    kl = lax.broadcasted_iota(jnp.int32, (MOBA_BLOCK, LANES), 0).astype(F32)
    nblk = qt_ref.shape[0]
    for bl in range(nblk):
        rows = slice(bl * MOBA_BLOCK, (bl + 1) * MOBA_BLOCK)
        qt_ref[bl] = q_t[:, rows]
        pieces = []
        for hh in range(N_HEADS):
            pieces += [v_t[hh * HEAD_DIM:(hh + 1) * HEAD_DIM, rows], ones_rows]
        va_ref[bl] = jnp.concatenate(pieces, axis=0).astype(BF16)
        blk_in_seq = (pl.program_id(0) * nblk + bl) % blocks_per_seq
        onehot = jnp.where(lane - AUX_BIAS == blk_in_seq, 1.0, 0.0)
        for hh in range(N_HEADS):
            pair = k[rows, (hh // 2) * LANES:(hh // 2 + 1) * LANES]
            kd = pair if hh % 2 == 0 else pltpu.roll(pair, HEAD_DIM, 1)
            k_aux = jnp.where(lane == AUX_ONE, ALIBI_SLOPES[hh] * kl, onehot)
            dst = slice(hh * LANES, (hh + 1) * LANES)
            ka_ref[rows, dst] = jnp.where(lane < HEAD_DIM, kd, k_aux).astype(BF16)
            km_ref[bl, :, dst] = jnp.mean(kd, axis=0, keepdims=True)


StreamGeometry = collections.namedtuple(
    "StreamGeometry", "db n_pages past_len nblocks rows t spp pps")


def _stream_geometry(page_table, cache, t, host_steps):
    db, n_pages = page_table.shape
    page = cache.shape[2]
    past_len = n_pages * page
    assert page == LANES and 2 * page == MOBA_BLOCK and past_len % MOBA_BLOCK == 0 and t == SUBLANES
    assert host_steps % db == 0 and n_pages % (host_steps // db) == 0
    spp = host_steps // db
    pps = n_pages // spp
    nblocks = past_len // MOBA_BLOCK
    assert pps % 2 == 0 and nblocks <= LANES
    return StreamGeometry(db, n_pages, past_len, nblocks, N_HEADS * t, t, spp, pps)


def _page_buffers(geo):
    return [pltpu.VMEM((2, geo.pps, D_ATTN, LANES), F32), pltpu.SemaphoreType.DMA((2,))]


def _page_copy(pt_ref, cache_ref, buf_ref, sem_ref, step, slot, n, pps):
    return pltpu.make_async_copy(cache_ref.at[pt_ref[step * pps + n]], buf_ref.at[slot, n], sem_ref.at[slot])


def _start_pages(pt_ref, cache_ref, buf_ref, sem_ref, step, n_steps, pps):
    slot = step % 2

    @pl.when(step == 0)
    def _():
        for n in range(pps):
            _page_copy(pt_ref, cache_ref, buf_ref, sem_ref, step, slot, n, pps).start()

    @pl.when(step + 1 < n_steps)
    def _():
        for n in range(pps):
            _page_copy(pt_ref, cache_ref, buf_ref, sem_ref, step + 1, 1 - slot, n, pps).start()


def _wait_pages(pt_ref, cache_ref, buf_ref, sem_ref, step, pps):
    slot = step % 2
    for n in range(pps):
        _page_copy(pt_ref, cache_ref, buf_ref, sem_ref, step, slot, n, pps).wait()
    return [buf_ref.at[slot, n] for n in range(pps)]


def _proj_keys_kernel(pt_ref, *refs, n_proj_out, blocks_per_seq, geo):
    n_in = 10
    proj_in, (q_ref, kn_ref, cache_ref) = refs[:n_in], refs[n_in:n_in + 3]
    outs = refs[n_in + 3:]
    proj_out = outs[:n_proj_out]
    pp_ref, po_ref, inv_ref, qbd_ref, s_ref, g_ref, buf_ref, sem_ref = outs[n_proj_out:]
    step = pl.program_id(0)
    _start_pages(pt_ref, cache_ref, buf_ref, sem_ref, step, pl.num_programs(0), geo.pps)
    _proj_kernel(*proj_in, *proj_out, prompt=True, blocks_per_seq=blocks_per_seq)
    pages = _wait_pages(pt_ref, cache_ref, buf_ref, sem_ref, step, geo.pps)
    c = step % geo.spp
    _skeys_body(c, c == geo.spp - 1, q_ref, kn_ref, pages, pp_ref, po_ref, inv_ref, qbd_ref, s_ref, g_ref,
                t=geo.t, nblocks=geo.nblocks, past_len=geo.past_len)


def _project(x2, g, ws, *, prompt, seq_len, keys=None):
    rows, d_model = x2.shape
    nblk = PROJ_BLOCKS if prompt else 1
    tm = nblk * MOBA_BLOCK
    assert rows % tm == 0
    inter = BF16 if prompt else F32
    row_spec = lambda width: pl.BlockSpec((tm, width), lambda i, *_: (i, 0))
    full = lambda a: pl.BlockSpec(a.shape, lambda i, *_: (0,) * a.ndim, pipeline_mode=pl.Buffered(1))
    sds = jax.ShapeDtypeStruct
    if prompt:
        assert seq_len % tm == 0 and seq_len // MOBA_BLOCK <= MAX_BLOCKS
        nb = seq_len // MOBA_BLOCK
        tps = seq_len // tm
        batch = rows // seq_len
        t_spec = pl.BlockSpec((None, D_ATTN, tm), lambda i, *_: (i // tps, 0, i % tps))
        blk_spec = lambda r: pl.BlockSpec((None, nblk, r, MOBA_BLOCK), lambda i, *_: (i // tps, i % tps, 0, 0))
        shapes = [sds((batch, D_ATTN, seq_len), F32), sds((batch, D_ATTN, seq_len), F32),
                  sds((rows, D_POOL), F32),
                  sds((batch, nb, D_ATTN, MOBA_BLOCK), BF16),
                  sds((rows, D_AUG), BF16),
                  sds((batch, nb, D_VT, MOBA_BLOCK), BF16)]
        specs = [t_spec, t_spec, row_spec(D_POOL), blk_spec(D_ATTN), row_spec(D_AUG), blk_spec(D_VT)]
    else:
        shapes = [sds((rows, D_ATTN), F32), sds((rows, D_ATTN), F32), sds((rows, D_POOL), F32),
                  sds((rows, D_ATTN), F32)]
        specs = [row_spec(D_ATTN), row_spec(D_ATTN), row_spec(D_POOL), row_spec(D_ATTN)]
    shapes += [sds((rows, D_ATTN), inter), sds((rows, D_POOL), inter),
               sds((rows, d_model), inter), sds((rows, d_model), inter)]
    specs += [row_spec(D_ATTN), row_spec(D_POOL), row_spec(d_model), row_spec(d_model)]
    if prompt:
        shapes += [sds((rows // MOBA_BLOCK, 1, D_AUG), F32)]
        specs += [pl.BlockSpec((nblk, 1, D_AUG), lambda i, *_: (i, 0, 0))]
    blocks_per_seq = (seq_len // MOBA_BLOCK) if prompt else 1
    in_specs = [row_spec(d_model), full(g)] + [full(w) for w in ws]
    if keys is None:
        return pl.pallas_call(
            functools.partial(_proj_kernel, prompt=prompt, blocks_per_seq=blocks_per_seq),
            out_shape=shapes, grid=(rows // tm,), in_specs=in_specs, out_specs=specs,
            compiler_params=_params("parallel"), name="proj_prompt" if prompt else "proj_sample",
        )(x2, g, *ws)

    q_s, kn_s, cache_k, page_table, t = keys
    geo = _stream_geometry(page_table, cache_k, t, rows // tm)
    seq_of = lambda i: i // geo.spp
    new_spec = pl.BlockSpec((t, D_ATTN), lambda i, pt_ref: (seq_of(i), 0))
    per_seq = pl.BlockSpec((None, geo.rows, LANES), lambda i, pt_ref: (seq_of(i), 0, 0))
    outs = pl.pallas_call(
        functools.partial(_proj_keys_kernel, n_proj_out=len(shapes), blocks_per_seq=blocks_per_seq, geo=geo),
        out_shape=shapes + [sds((geo.db, geo.nblocks, geo.rows, MOBA_BLOCK), BF16),
                            sds((geo.db, geo.rows, LANES), BF16), sds((geo.db, geo.rows, LANES), F32)],
        grid_spec=pltpu.PrefetchScalarGridSpec(
            num_scalar_prefetch=1, grid=(rows // tm,),
            in_specs=in_specs + [new_spec, new_spec, pl.BlockSpec(memory_space=pl.ANY)],
            out_specs=specs + [pl.BlockSpec((None, geo.nblocks, geo.rows, MOBA_BLOCK),
                                            lambda i, pt_ref: (seq_of(i), 0, 0, 0)), per_seq, per_seq],
            scratch_shapes=[pltpu.VMEM((geo.rows, D_ATTN), BF16),
                            pltpu.VMEM((geo.nblocks, geo.rows, MOBA_BLOCK), F32),
                            pltpu.VMEM((geo.rows, LANES), F32)] + _page_buffers(geo)),
        compiler_params=_params("arbitrary"), name="proj_prompt_keys",
    )(page_table.reshape(-1).astype(jnp.int32), x2, g, *ws, q_s, kn_s, cache_k)
    return outs[:len(shapes)], outs[len(shapes):]


def _select_bias(g, slope, i, nb):
    jidx = lax.broadcasted_iota(jnp.int32, g.shape, 0)
    own = jnp.zeros(g.shape, jnp.int32) + i
    work = jnp.where(jidx < own, g, -jnp.inf)
    sel = jidx == own
    for _ in range(MOBA_TOPK):
        best = jnp.max(work, axis=0, keepdims=True)
        at_best = (work == best) & (best > -jnp.inf)
        first = jnp.min(jnp.where(at_best, jidx, nb), axis=0, keepdims=True)
        pick = jidx == first
        sel = sel | pick
        work = jnp.where(pick, -jnp.inf, work)
    rel = ((jidx - own) * MOBA_BLOCK).astype(F32) * slope
    return jnp.where(sel, rel, NEG)


def _moba_schedule(items, nb):
    n, lag = len(items), MOBA_LAG
    steps = -(-(n + 2 * lag) // MOBA_UNROLL) * MOBA_UNROLL
    tab = np.zeros((5, steps), np.int32)
    for tau in range(steps):
        tab[1, tau], tab[0, tau] = items[tau] if tau < n else (nb - 1, 0)
        tab[2, tau] = items[tau - lag][0] if lag <= tau < n + lag else nb
        tab[4, tau], tab[3, tau] = items[tau - 2 * lag] if 2 * lag <= tau < n + 2 * lag else (nb, 0)
    return tab, steps


def _moba_kernel(own_ref, past_ref, qt_ref, k_ref, vt_ref, km_ref, al_ref, o_ref,
                 qs_ref, s_ref, p_ref, a_ref, x_ref, m_ref, acc_ref, cb_ref, *, nb, own_steps, past_steps):
    tq = MOBA_BLOCK
    key_pos = lax.broadcasted_iota(jnp.int32, (MOBA_BLOCK, tq), 0)
    qry_pos = lax.broadcasted_iota(jnp.int32, (MOBA_BLOCK, tq), 1)
    cb_ref[...] = jnp.where(key_pos <= qry_pos, 0.0, NEG)
    for ref in (s_ref, p_ref, a_ref, x_ref):
        ref[...] = jnp.zeros(ref.shape, ref.dtype)
    m_ref[nb] = jnp.zeros(m_ref.shape[1:], F32)
    acc_ref[nb] = jnp.zeros(acc_ref.shape[1:], F32)

    km = km_ref[...].astype(BF16)
    one_row = jnp.where(lax.broadcasted_iota(jnp.int32, (BF16_ROWS, tq), 0) == 0, 1.0, 0.0).astype(BF16)
    pad_rows = jnp.zeros((LANES - AUX_ONE - BF16_ROWS, tq), BF16)

    def gating(i, carry):
        qt = qt_ref[i]
        for e in range(MOBA_HEADS):
            qe = qt[e * HEAD_DIM:(e + 1) * HEAD_DIM]
            gate = _dot(km[:, e * LANES:e * LANES + HEAD_DIM], qe)
            slope = jnp.concatenate([al_ref[e:e + 1, :]] * (tq // LANES), axis=1)
            bias = _select_bias(gate, slope, i, nb)
            if nb < MAX_BLOCKS:
                bias = jnp.concatenate([bias, jnp.zeros((MAX_BLOCKS - nb, tq), F32)], axis=0)
            qs_ref[i, e] = jnp.concatenate([qe, bias.astype(BF16), one_row, pad_rows], axis=0)
        return carry

    lax.fori_loop(0, nb, gating, 0)

    def value_stage(tab_ref, tau, u, own):
        ring = u % MOBA_RING
        vblk = tab_ref[3, tau]
        slot = tab_ref[4, tau]
        for e in range(MOBA_HEADS):
            pv = _dot(vt_ref[vblk, e * V_ROWS:(e + 1) * V_ROWS, :], p_ref[ring, e])
            acc_ref[slot, e] = pv if own else a_ref[ring, e] * acc_ref[slot, e] + pv

    def softmax_stage(tab_ref, tau, u, own):
        ring = (u + MOBA_LAG) % MOBA_RING
        slot = tab_ref[2, tau]
        for e in range(MOBA_HEADS):
            m_new = x_ref[ring, e]
            if not own:
                m_prev = m_ref[slot, e]
                m_new = jnp.maximum(m_prev, m_new)
                a_ref[ring, e] = jnp.exp(m_prev - m_new)
            p_ref[ring, e] = jnp.exp((s_ref[ring, e] - m_new).astype(BF16))
            m_ref[slot, e] = m_new

    def score_stage(tab_ref, tau, u, own):
        ring = u % MOBA_RING
        r0 = pl.multiple_of(tab_ref[0, tau] * MOBA_BLOCK, MOBA_BLOCK)
        qblk = tab_ref[1, tau]
        for e in range(MOBA_HEADS):
            s = _dot(k_ref[pl.ds(r0, MOBA_BLOCK), e * LANES:(e + 1) * LANES], qs_ref[qblk, e])
            if own:
                s = s + cb_ref[...]
            s_ref[ring, e] = s
            x_ref[ring, e] = jnp.max(s, axis=0, keepdims=True)

    def pipeline(tab_ref, steps, own):
        def body(it, carry):
            for u in range(MOBA_UNROLL):
                for stage in (score_stage, value_stage, softmax_stage):
                    stage(tab_ref, MOBA_UNROLL * it + u, u, own)
            return carry

        lax.fori_loop(0, steps // MOBA_UNROLL, body, 0)

    pipeline(own_ref, own_steps, True)
    pipeline(past_ref, past_steps, False)

    def finalize(i, carry):
        outs = []
        for e in range(MOBA_HEADS):
            acc = acc_ref[i, e]
            outs.append(acc[0:HEAD_DIM] / acc[HEAD_DIM:HEAD_DIM + 1])
        q0 = pl.multiple_of(i * tq, tq)
        o_ref[pl.ds(q0, tq), :] = jnp.concatenate(outs, axis=0).T.astype(o_ref.dtype)
        return carry

    lax.fori_loop(0, nb, finalize, 0)


def _moba_prompt(qt, ka, va, km, batch, seq_len):
    nb = seq_len // MOBA_BLOCK
    assert nb <= MAX_BLOCKS and N_HEADS % MOBA_HEADS == 0
    groups = N_HEADS // MOBA_HEADS
    own_tab, own_steps = _moba_schedule([(i, i) for i in range(nb)], nb)
    past_tab, past_steps = _moba_schedule([(i, j) for i in range(nb) for j in range(i)], nb)
    slopes = jnp.broadcast_to(jnp.asarray(ALIBI_SLOPES, F32).reshape(groups, MOBA_HEADS, 1),
                              (groups, MOBA_HEADS, LANES))
    blk_spec = lambda r: pl.BlockSpec((None, nb, r, MOBA_BLOCK), lambda b, p, *_: (b, 0, p, 0))
    seq_spec = lambda w: pl.BlockSpec((None, seq_len, w), lambda b, p, *_: (b, 0, p))
    return pl.pallas_call(
        functools.partial(_moba_kernel, nb=nb, own_steps=own_steps, past_steps=past_steps),
        out_shape=jax.ShapeDtypeStruct((batch, seq_len, D_ATTN), BF16),
        grid_spec=pltpu.PrefetchScalarGridSpec(
            num_scalar_prefetch=2, grid=(batch, groups),
            in_specs=[blk_spec(MOBA_HEADS * HEAD_DIM), seq_spec(MOBA_HEADS * LANES),
                      blk_spec(MOBA_HEADS * V_ROWS),
                      pl.BlockSpec((None, nb, MOBA_HEADS * LANES), lambda b, p, *_: (b, 0, p)),
                      pl.BlockSpec((None, MOBA_HEADS, LANES), lambda b, p, *_: (p, 0, 0))],
            out_specs=seq_spec(MOBA_HEADS * HEAD_DIM),
            scratch_shapes=[pltpu.VMEM((nb, MOBA_HEADS, LANES, MOBA_BLOCK), BF16),
                            pltpu.VMEM((MOBA_RING, MOBA_HEADS, MOBA_BLOCK, MOBA_BLOCK), F32),
                            pltpu.VMEM((MOBA_RING, MOBA_HEADS, MOBA_BLOCK, MOBA_BLOCK), BF16),
                            pltpu.VMEM((MOBA_RING, MOBA_HEADS, 1, MOBA_BLOCK), F32),
                            pltpu.VMEM((MOBA_RING, MOBA_HEADS, 1, MOBA_BLOCK), F32),
                            pltpu.VMEM((nb + 1, MOBA_HEADS, 1, MOBA_BLOCK), F32),
                            pltpu.VMEM((nb + 1, MOBA_HEADS, V_ROWS, MOBA_BLOCK), F32),
                            pltpu.VMEM((MOBA_BLOCK, MOBA_BLOCK), F32)]),
        compiler_params=_params("parallel", "parallel"),
        name="moba_prompt",
    )(jnp.asarray(own_tab), jnp.asarray(past_tab), qt, ka.reshape(batch, seq_len, D_AUG), va,
      km.reshape(batch, nb, D_AUG), slopes)


def _head_rows(t):
    r = lax.broadcasted_iota(jnp.int32, (N_HEADS * t, 1), 0)
    return r // t, r % t


def _row_slopes(t):
    head, _ = _head_rows(t)
    slope = jnp.zeros((N_HEADS * t, 1), F32)
    for hh in range(N_HEADS):
        slope = jnp.where(head == hh, ALIBI_SLOPES[hh], slope)
    return slope


def _skeys_body(c, last, q_ref, kn_ref, pages, pp_ref, po_ref, inv_ref, qbd_ref, s_ref, g_ref,
                *, t, nblocks, past_len):
    rows = N_HEADS * t
    blocks_per_step = len(pages) // 2

    @pl.when(c == 0)
    def _():
        qt = jnp.concatenate([q_ref[...]] * N_HEADS, axis=0)
        head = lax.broadcasted_iota(jnp.int32, (rows, D_ATTN), 0) // t
        lane_head = lax.broadcasted_iota(jnp.int32, (rows, D_ATTN), 1) // HEAD_DIM
        qbd_ref[...] = jnp.where(head == lane_head, qt, 0.0).astype(BF16)
        g_ref[...] = jnp.zeros(g_ref.shape, F32)

    qbd = qbd_ref[...]
    g_lane = lax.broadcasted_iota(jnp.int32, (rows, LANES), 1)
    gate = g_ref[...]
    for bl in range(blocks_per_step):
        kt = jnp.concatenate([pages[2 * bl][...], pages[2 * bl + 1][...]], axis=1)
        j = c * blocks_per_step + bl
        s = _dot(qbd, kt.astype(BF16))
        s_ref[j] = s
        gate = jnp.where(g_lane == j, jnp.sum(s, axis=1, keepdims=True) * (1.0 / MOBA_BLOCK), gate)
    g_ref[...] = gate

    @pl.when(last)
    def _():
        rank = jnp.zeros(gate.shape, F32)
        for kk in range(nblocks):
            gk = gate[:, kk:kk + 1]
            rank = rank + jnp.where((gk > gate) | ((gk == gate) & (kk < g_lane)), 1.0, 0.0)
        unsel = jnp.where(rank < MOBA_TOPK, 0.0, NEG)
        slope = _row_slopes(t)
        _, qq = _head_rows(t)
        within = slope * lax.broadcasted_iota(jnp.int32, (rows, MOBA_BLOCK), 1).astype(F32)
        kn = jnp.concatenate([kn_ref[...], jnp.zeros((LANES - t, D_ATTN), F32)], axis=0).astype(BF16)
        r_own = lax.broadcasted_iota(jnp.int32, (rows, LANES), 1)
        s_own = jnp.where(r_own <= qq, _nt_dot(qbd, kn) + within[:, :LANES], NEG)
        m_run = jnp.full((rows, MOBA_BLOCK), NEG, F32)
        for j in range(nblocks):
            sj = s_ref[j] + within + (slope * float(j * MOBA_BLOCK - past_len) + unsel[:, j:j + 1])
            s_ref[j] = sj
            m_run = jnp.maximum(m_run, sj)
        m = jnp.maximum(jnp.max(m_run, axis=1, keepdims=True), jnp.max(s_own, axis=1, keepdims=True))
        p_own = jnp.exp(s_own - m)
        po_ref[...] = p_own.astype(po_ref.dtype)
        l_run = jnp.zeros((rows, MOBA_BLOCK), F32)
        for j in range(nblocks):
            pj = jnp.exp(s_ref[j] - m)
            pp_ref[j] = pj.astype(pp_ref.dtype)
            l_run = l_run + pj
        l = jnp.sum(l_run, axis=1, keepdims=True) + jnp.sum(p_own, axis=1, keepdims=True)
        inv_ref[...] = jnp.broadcast_to(1.0 / l, inv_ref.shape)


def _svals_body(c, last, pp_ref, po_ref, inv_ref, vn_ref, pages, o_ref, acc_ref, *, t):
    @pl.when(c == 0)
    def _():
        vn = jnp.concatenate([vn_ref[...], jnp.zeros((LANES - t, D_ATTN), F32)], axis=0).astype(BF16)
        acc_ref[...] = _dot(po_ref[...], vn)

    acc = acc_ref[...]
    for bl in range(len(pages) // 2):
        vt = jnp.concatenate([pages[2 * bl][...], pages[2 * bl + 1][...]], axis=1).astype(BF16)
        acc = acc + _nt_dot(pp_ref[bl], vt)
    acc_ref[...] = acc

    @pl.when(last)
    def _():
        scaled = acc * inv_ref[:, 0:1]
        lane_head = lax.broadcasted_iota(jnp.int32, (t, D_ATTN), 1) // HEAD_DIM
        out = jnp.zeros((t, D_ATTN), F32)
        for hh in range(N_HEADS):
            out = jnp.where(lane_head == hh, scaled[hh * t:(hh + 1) * t, :], out)
        o_ref[...] = out


def _merge_kernel(o_ref, sa_ref, u_ref, up_ref, sb_ref, ga_ref, gb_ref, x_ref,
                  wp_ref, ps_ref, wba_ref, wbp_ref, wo_ref, gp_ref, y_ref, ubuf_ref,
                  *, ns, tm, t0, zero_first_prev):
    i = pl.program_id(1)
    n = ns * tm
    prev = up_ref[...]
    if zero_first_prev:
        prev = jnp.where(i == 0, 0.0, prev)
    ubuf_ref[:, 0:PREV_ROWS, :] = prev
    ubuf_ref[:, PREV_ROWS:PREV_ROWS + tm, :] = u_ref[...]
    pos = t0 + i * tm + lax.broadcasted_iota(jnp.int32, (ns, tm, POOL_GROUP), 1)
    mixed = []
    for gidx, w in enumerate(POOL_WINDOWS):
        gl = slice(gidx * POOL_GROUP, (gidx + 1) * POOL_GROUP)
        cur = ubuf_ref[:, PREV_ROWS:PREV_ROWS + tm, gl]
        tot = cur
        for s in range(1, w):
            tot = tot + ubuf_ref[:, PREV_ROWS - s:PREV_ROWS - s + tm, gl]
        cnt = jnp.minimum(pos + 1, w).astype(F32)
        pooled = (tot / cnt - cur).reshape(n, POOL_GROUP)
        mixed.append(_dot(pooled.astype(BF16), wp_ref[gidx]))
    mixed = jnp.concatenate(mixed, axis=1) * ps_ref[...]

    def flat(ref):
        return ref[...].reshape(n, ref.shape[-1]).astype(F32)

    o_b = (mixed * flat(sb_ref)).astype(BF16)
    o_a = (flat(o_ref) * flat(sa_ref)).astype(BF16)
    m_a = _dot(o_a, wba_ref[...])
    m_b = _dot(o_b, wbp_ref[...])
    merged = flat(ga_ref) * m_a + flat(gb_ref) * m_b
    z = _dot(merged.astype(BF16), wo_ref[...])
    ms = jnp.mean(z * z, axis=-1, keepdims=True)
    y = flat(x_ref) + z * lax.rsqrt(ms + RMS_EPS) * gp_ref[...]
    y_ref[...] = y.reshape(y_ref.shape)


def _merge_vals_kernel(pt_ref, *refs, geo, merge_kw):
    n_in = 14
    merge_in, (pp_ref, po_ref, inv_ref, vn_ref, cache_ref) = refs[:n_in], refs[n_in:n_in + 5]
    y_ref, os_ref, ubuf_ref, acc_ref, buf_ref, sem_ref = refs[n_in + 5:]
    step = pl.program_id(0) * pl.num_programs(1) + pl.program_id(1)
    _start_pages(pt_ref, cache_ref, buf_ref, sem_ref, step, pl.num_programs(0) * pl.num_programs(1), geo.pps)
    _merge_kernel(*merge_in, y_ref, ubuf_ref, **merge_kw)
    pages = _wait_pages(pt_ref, cache_ref, buf_ref, sem_ref, step, geo.pps)
    c = step % geo.spp
    _svals_body(c, c == geo.spp - 1, pp_ref, po_ref, inv_ref, vn_ref, pages, os_ref, acc_ref, t=geo.t)


def _merge(o, sa, u, u_prev, sb, ga, gb, x, ws, *, ns, tm, t0, zero_first_prev, vals=None):
    nseq, seq_len, d_model = x.shape
    assert nseq % ns == 0 and seq_len % tm == 0 and tm % SUBLANES == 0
    assert ns == 1 or (tm == SUBLANES and all(a.dtype == F32 for a in (o, sa, sb, ga, gb)))
    tile = lambda width: pl.BlockSpec((ns, tm, width), lambda b, i, *_: (b, i, 0))
    full = lambda a: pl.BlockSpec(a.shape, lambda b, i, *_: (0,) * a.ndim, pipeline_mode=pl.Buffered(1))
    per = tm // PREV_ROWS if tm >= PREV_ROWS else 0
    prev_spec = pl.BlockSpec((ns, PREV_ROWS, D_POOL),
                             lambda b, i, *_: (b, jnp.maximum(i * per - 1, 0), 0))
    merge_kw = dict(ns=ns, tm=tm, t0=t0, zero_first_prev=zero_first_prev)
    in_specs = [tile(D_ATTN), tile(D_ATTN), tile(D_POOL), prev_spec, tile(D_POOL),
                tile(d_model), tile(d_model), tile(d_model)] + [full(w) for w in ws]
    ubuf = pltpu.VMEM((ns, PREV_ROWS + tm, D_POOL), F32)
    grid = (nseq // ns, seq_len // tm)
    if vals is None:
        return pl.pallas_call(
            functools.partial(_merge_kernel, **merge_kw), out_shape=jax.ShapeDtypeStruct(x.shape, F32),
            grid=grid, in_specs=in_specs, out_specs=tile(d_model), scratch_shapes=[ubuf],
            compiler_params=_params("parallel", "arbitrary"),
            name="merge_prompt" if ns == 1 else "merge_sample",
        )(o, sa, u, u_prev, sb, ga, gb, x, *ws)

    pp, po, inv, vn, cache_v, page_table, t = vals
    geo = _stream_geometry(page_table, cache_v, t, grid[0] * grid[1])
    step = lambda b, i: b * grid[1] + i
    seq_of = lambda b, i: step(b, i) // geo.spp
    new_spec = pl.BlockSpec((t, D_ATTN), lambda b, i, pt_ref: (seq_of(b, i), 0))
    per_seq = pl.BlockSpec((None, geo.rows, LANES), lambda b, i, pt_ref: (seq_of(b, i), 0, 0))
    return pl.pallas_call(
        functools.partial(_merge_vals_kernel, geo=geo, merge_kw=merge_kw),
        out_shape=[jax.ShapeDtypeStruct(x.shape, F32), jax.ShapeDtypeStruct((geo.db * t, D_ATTN), F32)],
        grid_spec=pltpu.PrefetchScalarGridSpec(
            num_scalar_prefetch=1, grid=grid,
            in_specs=in_specs + [pl.BlockSpec((None, geo.pps // 2, geo.rows, MOBA_BLOCK),
                                              lambda b, i, pt_ref: (seq_of(b, i), step(b, i) % geo.spp, 0, 0)),
                                 per_seq, per_seq, new_spec, pl.BlockSpec(memory_space=pl.ANY)],
            out_specs=[tile(d_model), new_spec],
            scratch_shapes=[ubuf, pltpu.VMEM((geo.rows, D_ATTN), F32)] + _page_buffers(geo)),
        compiler_params=_params("arbitrary", "arbitrary"), name="merge_prompt_values",
    )(page_table.reshape(-1).astype(jnp.int32), o, sa, u, u_prev, sb, ga, gb, x, *ws,
      pp, po, inv, vn, cache_v)


def _layer(yp, ys, cache_k, cache_v, state, page_table, g_pre, w_in, w_pool, pool_scale,
           w_br_attn, w_br_pool, w_out, g_post):
    batch, seq_len, d_model = yp.shape
    db, t, _ = ys.shape
    past_len = page_table.shape[1] * cache_k.shape[2]

    bounds = [0]
    for width in (D_ATTN, D_ATTN, D_ATTN, D_ATTN, D_POOL, D_POOL, d_model, d_model):
        bounds.append(bounds[-1] + width)
    wq, wk, wv, wza, wu, wzb, wga, wgb = [w_in[:, a:b] for a, b in zip(bounds[:-1], bounds[1:])]
    proj_ws = [w.astype(BF16) for w in (wq * HEAD_DIM ** -0.5, wk, wv, wza, wu, wzb, wga, wgb)]
    g_pre2 = g_pre.reshape(1, d_model)
    merge_ws = [w_pool.astype(BF16), pool_scale.reshape(1, D_POOL), w_br_attn.astype(BF16),
                w_br_pool.astype(BF16), w_out.astype(BF16), g_post.reshape(1, d_model)]

    ks, vs, us, qs, sas, sbs, gas, gbs = _project(
        ys.reshape(db * t, d_model), g_pre2, proj_ws, prompt=False, seq_len=t)

    (kt, vt, up, qt, ka, va, sa, sb, ga, gb, km), (pp, po, inv) = _project(
        yp.reshape(batch * seq_len, d_model), g_pre2, proj_ws, prompt=True, seq_len=seq_len,
        keys=(qs, ks, cache_k, page_table, t))
    op = _moba_prompt(qt, ka, va, km, batch, seq_len)
    r3 = lambda a: a.reshape(batch, seq_len, a.shape[-1])
    up3 = r3(up)
    y_prompt, osamp = _merge(op, r3(sa), up3, up3, r3(sb), r3(ga), r3(gb), yp, merge_ws,
                             ns=1, tm=MOBA_BLOCK, t0=0, zero_first_prev=True,
                             vals=(pp, po, inv, vs, cache_v, page_table, t))
    seq_major = lambda a: jnp.transpose(a.reshape(batch, N_HEADS, HEAD_DIM, seq_len), (0, 3, 1, 2))

    s3 = lambda a: a.reshape(db, t, a.shape[-1])
    us3 = s3(us)
    state16 = jnp.concatenate([jnp.zeros((db, PREV_ROWS - POOL_STATE, D_POOL), F32), state], axis=1)
    y_sample = _merge(s3(osamp), s3(sas), us3, state16, s3(sbs), s3(gas), s3(gbs), ys, merge_ws,
                      ns=16, tm=t, t0=past_len, zero_first_prev=False)

    pool_prompt = up3[:, seq_len - POOL_STATE:]
    pool_sample = jnp.concatenate([state, us3], axis=1)[:, -POOL_STATE:]
    return (y_prompt, y_sample, seq_major(kt), seq_major(vt), pool_prompt,
            ks.reshape(db, t, N_HEADS, HEAD_DIM), vs.reshape(db, t, N_HEADS, HEAD_DIM), pool_sample)


def kernel(x_prompt, x_sample, cache_k, cache_v, state_pool, page_table, g_pre, w_in, w_pool,
           pool_scale, w_br_attn, w_br_pool, w_out, g_post):
    depth = w_in.shape[0]
    n_phys, page = cache_k.shape[1], cache_k.shape[2]
    pages_t = lambda c: jnp.transpose(c, (0, 2, 3, 1)).reshape(n_phys, D_ATTN, page)
    yp, ys = x_prompt, x_sample
    per_layer = []
    for l in range(depth):
        outs = _layer(yp, ys, pages_t(cache_k[l]), pages_t(cache_v[l]),
                      state_pool[l], page_table, g_pre[l], w_in[l], w_pool[l], pool_scale[l],
                      w_br_attn[l], w_br_pool[l], w_out[l], g_post[l])
        yp, ys = outs[0], outs[1]
        per_layer.append(outs[2:])
    stacked = [jnp.stack([layer[i] for layer in per_layer]) for i in range(6)]
    return (yp, ys, *stacked)
```

```python
import collections
import functools

import numpy as np
import jax
import jax.numpy as jnp
from jax import lax
from jax.experimental import pallas as pl
from jax.experimental.pallas import tpu as pltpu

F32 = jnp.float32
BF16 = jnp.bfloat16

N_HEADS = 8
HEAD_DIM = 64
D_ATTN = N_HEADS * HEAD_DIM
MOBA_BLOCK = 256
MOBA_TOPK = 3
POOL_WINDOWS = (2, 4, 8, 16)
POOL_GROUP = 128
D_POOL = POOL_GROUP * len(POOL_WINDOWS)
POOL_STATE = max(POOL_WINDOWS) - 1
PREV_ROWS = 16
RMS_EPS = 1e-6
LANES = 128
SUBLANES = 8
BF16_ROWS = 16
NEG = -1e30
ALIBI_SLOPES = tuple(2.0 ** (-(h + 1)) for h in range(N_HEADS))

AUX_BIAS = HEAD_DIM
MAX_BLOCKS = 16
AUX_ONE = AUX_BIAS + MAX_BLOCKS
D_AUG = N_HEADS * LANES
V_ROWS = HEAD_DIM + BF16_ROWS
D_VT = N_HEADS * V_ROWS

MOBA_HEADS = 4
MOBA_LAG = 2
MOBA_RING = 2 * MOBA_LAG
MOBA_UNROLL = MOBA_RING
VMEM_LIMIT = 56 * 1024 * 1024
PROJ_BLOCKS = 1


def _nt_dot(a, b):
    return lax.dot_general(a, b, (((1,), (1,)), ((), ())), preferred_element_type=F32)


def _dot(a, b):
    return jnp.dot(a, b, preferred_element_type=F32)


def _sigmoid(z):
    return 1.0 / (1.0 + jnp.exp(-z))


def _params(*semantics):
    return pltpu.CompilerParams(dimension_semantics=semantics, vmem_limit_bytes=VMEM_LIMIT)


def _proj_kernel(x_ref, g_ref, wq_ref, wk_ref, wv_ref, wza_ref, wu_ref, wzb_ref, wga_ref, wgb_ref,
                 *outs, prompt, blocks_per_seq):
    x = x_ref[...]
    ms = jnp.mean(x * x, axis=-1, keepdims=True)
    h = (x * lax.rsqrt(ms + RMS_EPS) * g_ref[...]).astype(BF16)

    def proj(w_ref):
        return _dot(h, w_ref[...])

    q = proj(wq_ref)
    k = proj(wk_ref)
    v = proj(wv_ref)
    za = proj(wza_ref)
    zb = proj(wzb_ref)
    if prompt:
        kt_ref, vt_ref, u_ref, qt_ref, ka_ref, va_ref, sa_ref, sb_ref, ga_ref, gb_ref, km_ref = outs
    else:
        k_ref, v_ref, u_ref, q_ref, sa_ref, sb_ref, ga_ref, gb_ref = outs
        q_ref[...] = q
        k_ref[...] = k
        v_ref[...] = v
    u_ref[...] = proj(wu_ref)
    sa_ref[...] = (za * _sigmoid(za)).astype(sa_ref.dtype)
    sb_ref[...] = (zb * _sigmoid(zb)).astype(sb_ref.dtype)
    ga_ref[...] = _sigmoid(proj(wga_ref)).astype(ga_ref.dtype)
    gb_ref[...] = _sigmoid(proj(wgb_ref)).astype(gb_ref.dtype)
    if not prompt:
        return

    v_t = v.T
    q_t = q.T.astype(BF16)
    kt_ref[...] = k.T
    vt_ref[...] = v_t
    ones_rows = jnp.where(lax.broadcasted_iota(jnp.int32, (BF16_ROWS, MOBA_BLOCK), 0) == 0, 1.0, 0.0)
    lane = lax.broadcasted_iota(jnp.int32, (MOBA_BLOCK, LANES), 1)
    kl = lax.broadcasted_iota(jnp.int32, (MOBA_BLOCK, LANES), 0).astype(F32)
    nblk = qt_ref.shape[0]
    for bl in range(nblk):
        rows = slice(bl * MOBA_BLOCK, (bl + 1) * MOBA_BLOCK)
        qt_ref[bl] = q_t[:, rows]
        pieces = []
        for hh in range(N_HEADS):
            pieces += [v_t[hh * HEAD_DIM:(hh + 1) * HEAD_DIM, rows], ones_rows]
        va_ref[bl] = jnp.concatenate(pieces, axis=0).astype(BF16)
        blk_in_seq = (pl.program_id(0) * nblk + bl) % blocks_per_seq
        onehot = jnp.where(lane - AUX_BIAS == blk_in_seq, 1.0, 0.0)
        for hh in range(N_HEADS):
            pair = k[rows, (hh // 2) * LANES:(hh // 2 + 1) * LANES]
            kd = pair if hh % 2 == 0 else pltpu.roll(pair, HEAD_DIM, 1)
            k_aux = jnp.where(lane == AUX_ONE, ALIBI_SLOPES[hh] * kl, onehot)
            dst = slice(hh * LANES, (hh + 1) * LANES)
            ka_ref[rows, dst] = jnp.where(lane < HEAD_DIM, kd, k_aux).astype(BF16)
            km_ref[bl, :, dst] = jnp.mean(kd, axis=0, keepdims=True)


StreamGeometry = collections.namedtuple(
    "StreamGeometry", "db n_pages past_len nblocks rows t spp pps")


def _stream_geometry(page_table, cache, t, host_steps):
    db, n_pages = page_table.shape
    page = cache.shape[2]
    past_len = n_pages * page
    assert page == LANES and 2 * page == MOBA_BLOCK and past_len % MOBA_BLOCK == 0 and t == SUBLANES
    assert host_steps % db == 0 and n_pages % (host_steps // db) == 0
    spp = host_steps // db
    pps = n_pages // spp
    nblocks = past_len // MOBA_BLOCK
    assert pps % 2 == 0 and nblocks <= LANES
    return StreamGeometry(db, n_pages, past_len, nblocks, N_HEADS * t, t, spp, pps)


def _page_buffers(geo):
    return [pltpu.VMEM((2, geo.pps, D_ATTN, LANES), F32), pltpu.SemaphoreType.DMA((2,))]


def _page_copy(pt_ref, cache_ref, buf_ref, sem_ref, step, slot, n, pps):
    return pltpu.make_async_copy(cache_ref.at[pt_ref[step * pps + n]], buf_ref.at[slot, n], sem_ref.at[slot])


def _start_pages(pt_ref, cache_ref, buf_ref, sem_ref, step, n_steps, pps):
    slot = step % 2

    @pl.when(step == 0)
    def _():
        for n in range(pps):
            _page_copy(pt_ref, cache_ref, buf_ref, sem_ref, step, slot, n, pps).start()

    @pl.when(step + 1 < n_steps)
    def _():
        for n in range(pps):
            _page_copy(pt_ref, cache_ref, buf_ref, sem_ref, step + 1, 1 - slot, n, pps).start()


def _wait_pages(pt_ref, cache_ref, buf_ref, sem_ref, step, pps):
    slot = step % 2
    for n in range(pps):
        _page_copy(pt_ref, cache_ref, buf_ref, sem_ref, step, slot, n, pps).wait()
    return [buf_ref.at[slot, n] for n in range(pps)]


def _proj_keys_kernel(pt_ref, *refs, n_proj_out, blocks_per_seq, geo):
    n_in = 10
    proj_in, (q_ref, kn_ref, cache_ref) = refs[:n_in], refs[n_in:n_in + 3]
    outs = refs[n_in + 3:]
    proj_out = outs[:n_proj_out]
    pp_ref, po_ref, inv_ref, qbd_ref, s_ref, g_ref, buf_ref, sem_ref = outs[n_proj_out:]
    step = pl.program_id(0)
    _start_pages(pt_ref, cache_ref, buf_ref, sem_ref, step, pl.num_programs(0), geo.pps)
    pages = _wait_pages(pt_ref, cache_ref, buf_ref, sem_ref, step, geo.pps)
    c = step % geo.spp
    _skeys_body(c, c == geo.spp - 1, q_ref, kn_ref, pages, pp_ref, po_ref, inv_ref, qbd_ref, s_ref, g_ref,
                t=geo.t, nblocks=geo.nblocks, past_len=geo.past_len,
                host=lambda: _proj_kernel(*proj_in, *proj_out, prompt=True, blocks_per_seq=blocks_per_seq))


def _project(x2, g, ws, *, prompt, seq_len, keys=None):
    rows, d_model = x2.shape
    nblk = PROJ_BLOCKS if prompt else 1
    tm = nblk * MOBA_BLOCK
    assert rows % tm == 0
    inter = BF16 if prompt else F32
    row_spec = lambda width: pl.BlockSpec((tm, width), lambda i, *_: (i, 0))
    full = lambda a: pl.BlockSpec(a.shape, lambda i, *_: (0,) * a.ndim, pipeline_mode=pl.Buffered(1))
    sds = jax.ShapeDtypeStruct
    if prompt:
        assert seq_len % tm == 0 and seq_len // MOBA_BLOCK <= MAX_BLOCKS
        nb = seq_len // MOBA_BLOCK
        tps = seq_len // tm
        batch = rows // seq_len
        t_spec = pl.BlockSpec((None, D_ATTN, tm), lambda i, *_: (i // tps, 0, i % tps))
        blk_spec = lambda r: pl.BlockSpec((None, nblk, r, MOBA_BLOCK), lambda i, *_: (i // tps, i % tps, 0, 0))
        shapes = [sds((batch, D_ATTN, seq_len), F32), sds((batch, D_ATTN, seq_len), F32),
                  sds((rows, D_POOL), F32),
                  sds((batch, nb, D_ATTN, MOBA_BLOCK), BF16),
                  sds((rows, D_AUG), BF16),
                  sds((batch, nb, D_VT, MOBA_BLOCK), BF16)]
        specs = [t_spec, t_spec, row_spec(D_POOL), blk_spec(D_ATTN), row_spec(D_AUG), blk_spec(D_VT)]
    else:
        shapes = [sds((rows, D_ATTN), F32), sds((rows, D_ATTN), F32), sds((rows, D_POOL), F32),
                  sds((rows, D_ATTN), F32)]
        specs = [row_spec(D_ATTN), row_spec(D_ATTN), row_spec(D_POOL), row_spec(D_ATTN)]
    shapes += [sds((rows, D_ATTN), inter), sds((rows, D_POOL), inter),
               sds((rows, d_model), inter), sds((rows, d_model), inter)]
    specs += [row_spec(D_ATTN), row_spec(D_POOL), row_spec(d_model), row_spec(d_model)]
    if prompt:
        shapes += [sds((rows // MOBA_BLOCK, 1, D_AUG), F32)]
        specs += [pl.BlockSpec((nblk, 1, D_AUG), lambda i, *_: (i, 0, 0))]
    blocks_per_seq = (seq_len // MOBA_BLOCK) if prompt else 1
    in_specs = [row_spec(d_model), full(g)] + [full(w) for w in ws]
    if keys is None:
        return pl.pallas_call(
            functools.partial(_proj_kernel, prompt=prompt, blocks_per_seq=blocks_per_seq),
            out_shape=shapes, grid=(rows // tm,), in_specs=in_specs, out_specs=specs,
            compiler_params=_params("parallel"), name="proj_prompt" if prompt else "proj_sample",
        )(x2, g, *ws)

    q_s, kn_s, cache_k, page_table, t = keys
    geo = _stream_geometry(page_table, cache_k, t, rows // tm)
    seq_of = lambda i: i // geo.spp
    new_spec = pl.BlockSpec((t, D_ATTN), lambda i, pt_ref: (seq_of(i), 0))
    per_seq = pl.BlockSpec((None, geo.rows, LANES), lambda i, pt_ref: (seq_of(i), 0, 0))
    outs = pl.pallas_call(
        functools.partial(_proj_keys_kernel, n_proj_out=len(shapes), blocks_per_seq=blocks_per_seq, geo=geo),
        out_shape=shapes + [sds((geo.db, geo.nblocks, geo.rows, MOBA_BLOCK), BF16),
                            sds((geo.db, geo.rows, LANES), BF16), sds((geo.db, geo.rows, LANES), F32)],
        grid_spec=pltpu.PrefetchScalarGridSpec(
            num_scalar_prefetch=1, grid=(rows // tm,),
            in_specs=in_specs + [new_spec, new_spec, pl.BlockSpec(memory_space=pl.ANY)],
            out_specs=specs + [pl.BlockSpec((None, geo.nblocks, geo.rows, MOBA_BLOCK),
                                            lambda i, pt_ref: (seq_of(i), 0, 0, 0)), per_seq, per_seq],
            scratch_shapes=[pltpu.VMEM((geo.rows, D_ATTN), BF16),
                            pltpu.VMEM((geo.nblocks, geo.rows, MOBA_BLOCK), F32),
                            pltpu.VMEM((geo.rows, LANES), F32)] + _page_buffers(geo)),
        compiler_params=_params("arbitrary"), name="proj_prompt_keys",
    )(page_table.reshape(-1).astype(jnp.int32), x2, g, *ws, q_s, kn_s, cache_k)
    return outs[:len(shapes)], outs[len(shapes):]


def _select_bias(g, slope, i, nb):
    jidx = lax.broadcasted_iota(jnp.int32, g.shape, 0)
    own = jnp.zeros(g.shape, jnp.int32) + i
    work = jnp.where(jidx < own, g, -jnp.inf)
    sel = jidx == own
    for _ in range(MOBA_TOPK):
        best = jnp.max(work, axis=0, keepdims=True)
        at_best = (work == best) & (best > -jnp.inf)
        first = jnp.min(jnp.where(at_best, jidx, nb), axis=0, keepdims=True)
        pick = jidx == first
        sel = sel | pick
        work = jnp.where(pick, -jnp.inf, work)
    rel = ((jidx - own) * MOBA_BLOCK).astype(F32) * slope
    return jnp.where(sel, rel, NEG)


def _moba_schedule(items, nb):
    n, lag = len(items), MOBA_LAG
    steps = -(-(n + 2 * lag) // MOBA_UNROLL) * MOBA_UNROLL
    tab = np.zeros((5, steps), np.int32)
    for tau in range(steps):
        tab[1, tau], tab[0, tau] = items[tau] if tau < n else (nb - 1, 0)
        tab[2, tau] = items[tau - lag][0] if lag <= tau < n + lag else nb
        tab[4, tau], tab[3, tau] = items[tau - 2 * lag] if 2 * lag <= tau < n + 2 * lag else (nb, 0)
    return tab, steps


def _moba_kernel(own_ref, past_ref, qt_ref, k_ref, vt_ref, km_ref, al_ref, o_ref,
                 qs_ref, s_ref, p_ref, a_ref, x_ref, m_ref, acc_ref, cb_ref, *, nb, own_steps, past_steps):
    tq = MOBA_BLOCK
    key_pos = lax.broadcasted_iota(jnp.int32, (MOBA_BLOCK, tq), 0)
    qry_pos = lax.broadcasted_iota(jnp.int32, (MOBA_BLOCK, tq), 1)
    cb_ref[...] = jnp.where(key_pos <= qry_pos, 0.0, NEG)
    for ref in (s_ref, p_ref, a_ref, x_ref):
        ref[...] = jnp.zeros(ref.shape, ref.dtype)
    m_ref[nb] = jnp.zeros(m_ref.shape[1:], F32)
    acc_ref[nb] = jnp.zeros(acc_ref.shape[1:], F32)

    km = km_ref[...].astype(BF16)
    one_row = jnp.where(lax.broadcasted_iota(jnp.int32, (BF16_ROWS, tq), 0) == 0, 1.0, 0.0).astype(BF16)
    pad_rows = jnp.zeros((LANES - AUX_ONE - BF16_ROWS, tq), BF16)

    def gating(i, carry):
        qt = qt_ref[i]
        for e in range(MOBA_HEADS):
            qe = qt[e * HEAD_DIM:(e + 1) * HEAD_DIM]
            gate = _dot(km[:, e * LANES:e * LANES + HEAD_DIM], qe)
            slope = jnp.concatenate([al_ref[e:e + 1, :]] * (tq // LANES), axis=1)
            bias = _select_bias(gate, slope, i, nb)
            if nb < MAX_BLOCKS:
                bias = jnp.concatenate([bias, jnp.zeros((MAX_BLOCKS - nb, tq), F32)], axis=0)
            qs_ref[i, e] = jnp.concatenate([qe, bias.astype(BF16), one_row, pad_rows], axis=0)
        return carry

    lax.fori_loop(0, nb, gating, 0)

    def value_stage(tab_ref, tau, u, own):
        ring = u % MOBA_RING
        vblk = tab_ref[3, tau]
        slot = tab_ref[4, tau]
        for e in range(MOBA_HEADS):
            pv = _dot(vt_ref[vblk, e * V_ROWS:(e + 1) * V_ROWS, :], p_ref[ring, e])
            acc_ref[slot, e] = pv if own else a_ref[ring, e] * acc_ref[slot, e] + pv

    def softmax_stage(tab_ref, tau, u, own):
        ring = (u + MOBA_LAG) % MOBA_RING
        slot = tab_ref[2, tau]
        for e in range(MOBA_HEADS):
            m_new = x_ref[ring, e]
            if not own:
                m_prev = m_ref[slot, e]
                m_new = jnp.maximum(m_prev, m_new)
                a_ref[ring, e] = jnp.exp(m_prev - m_new)
            p_ref[ring, e] = jnp.exp((s_ref[ring, e] - m_new).astype(BF16))
            m_ref[slot, e] = m_new

    def score_stage(tab_ref, tau, u, own):
        ring = u % MOBA_RING
        r0 = pl.multiple_of(tab_ref[0, tau] * MOBA_BLOCK, MOBA_BLOCK)
        qblk = tab_ref[1, tau]
        for e in range(MOBA_HEADS):
            s = _dot(k_ref[pl.ds(r0, MOBA_BLOCK), e * LANES:(e + 1) * LANES], qs_ref[qblk, e])
            if own:
                s = s + cb_ref[...]
            s_ref[ring, e] = s
            x_ref[ring, e] = jnp.max(s, axis=0, keepdims=True)

    def pipeline(tab_ref, steps, own):
        def body(it, carry):
            for u in range(MOBA_UNROLL):
                for stage in (score_stage, value_stage, softmax_stage):
                    stage(tab_ref, MOBA_UNROLL * it + u, u, own)
            return carry

        lax.fori_loop(0, steps // MOBA_UNROLL, body, 0)

    pipeline(own_ref, own_steps, True)
    pipeline(past_ref, past_steps, False)

    def finalize(i, carry):
        outs = []
        for e in range(MOBA_HEADS):
            acc = acc_ref[i, e]
            outs.append(acc[0:HEAD_DIM] / acc[HEAD_DIM:HEAD_DIM + 1])
        q0 = pl.multiple_of(i * tq, tq)
        o_ref[pl.ds(q0, tq), :] = jnp.concatenate(outs, axis=0).T.astype(o_ref.dtype)
        return carry

    lax.fori_loop(0, nb, finalize, 0)


def _moba_prompt(qt, ka, va, km, batch, seq_len):
    nb = seq_len // MOBA_BLOCK
    assert nb <= MAX_BLOCKS and N_HEADS % MOBA_HEADS == 0
    groups = N_HEADS // MOBA_HEADS
    own_tab, own_steps = _moba_schedule([(i, i) for i in range(nb)], nb)
    past_tab, past_steps = _moba_schedule([(i, j) for i in range(nb) for j in range(i)], nb)
    slopes = jnp.broadcast_to(jnp.asarray(ALIBI_SLOPES, F32).reshape(groups, MOBA_HEADS, 1),
                              (groups, MOBA_HEADS, LANES))
    blk_spec = lambda r: pl.BlockSpec((None, nb, r, MOBA_BLOCK), lambda b, p, *_: (b, 0, p, 0))
    seq_spec = lambda w: pl.BlockSpec((None, seq_len, w), lambda b, p, *_: (b, 0, p))
    return pl.pallas_call(
        functools.partial(_moba_kernel, nb=nb, own_steps=own_steps, past_steps=past_steps),
        out_shape=jax.ShapeDtypeStruct((batch, seq_len, D_ATTN), BF16),
        grid_spec=pltpu.PrefetchScalarGridSpec(
            num_scalar_prefetch=2, grid=(batch, groups),
            in_specs=[blk_spec(MOBA_HEADS * HEAD_DIM), seq_spec(MOBA_HEADS * LANES),
                      blk_spec(MOBA_HEADS * V_ROWS),
                      pl.BlockSpec((None, nb, MOBA_HEADS * LANES), lambda b, p, *_: (b, 0, p)),
                      pl.BlockSpec((None, MOBA_HEADS, LANES), lambda b, p, *_: (p, 0, 0))],
            out_specs=seq_spec(MOBA_HEADS * HEAD_DIM),
            scratch_shapes=[pltpu.VMEM((nb, MOBA_HEADS, LANES, MOBA_BLOCK), BF16),
                            pltpu.VMEM((MOBA_RING, MOBA_HEADS, MOBA_BLOCK, MOBA_BLOCK), F32),
                            pltpu.VMEM((MOBA_RING, MOBA_HEADS, MOBA_BLOCK, MOBA_BLOCK), BF16),
                            pltpu.VMEM((MOBA_RING, MOBA_HEADS, 1, MOBA_BLOCK), F32),
                            pltpu.VMEM((MOBA_RING, MOBA_HEADS, 1, MOBA_BLOCK), F32),
                            pltpu.VMEM((nb + 1, MOBA_HEADS, 1, MOBA_BLOCK), F32),
                            pltpu.VMEM((nb + 1, MOBA_HEADS, V_ROWS, MOBA_BLOCK), F32),
                            pltpu.VMEM((MOBA_BLOCK, MOBA_BLOCK), F32)]),
        compiler_params=_params("parallel", "parallel"),
        name="moba_prompt",
    )(jnp.asarray(own_tab), jnp.asarray(past_tab), qt, ka.reshape(batch, seq_len, D_AUG), va,
      km.reshape(batch, nb, D_AUG), slopes)


def _head_rows(t):
    r = lax.broadcasted_iota(jnp.int32, (N_HEADS * t, 1), 0)
    return r // t, r % t


def _row_slopes(t):
    head, _ = _head_rows(t)
    slope = jnp.zeros((N_HEADS * t, 1), F32)
    for hh in range(N_HEADS):
        slope = jnp.where(head == hh, ALIBI_SLOPES[hh], slope)
    return slope


def _skeys_body(c, last, q_ref, kn_ref, pages, pp_ref, po_ref, inv_ref, qbd_ref, s_ref, g_ref,
                *, t, nblocks, past_len, host=None):
    rows = N_HEADS * t
    blocks_per_step = len(pages) // 2

    @pl.when(c == 0)
    def _():
        qt = jnp.concatenate([q_ref[...]] * N_HEADS, axis=0)
        head = lax.broadcasted_iota(jnp.int32, (rows, D_ATTN), 0) // t
        lane_head = lax.broadcasted_iota(jnp.int32, (rows, D_ATTN), 1) // HEAD_DIM
        qbd_ref[...] = jnp.where(head == lane_head, qt, 0.0).astype(BF16)
        g_ref[...] = jnp.zeros(g_ref.shape, F32)

    if host is not None:
        host()
    qbd = qbd_ref[...]
    g_lane = lax.broadcasted_iota(jnp.int32, (rows, LANES), 1)
    gate = g_ref[...]
    for bl in range(blocks_per_step):
        kt = jnp.concatenate([pages[2 * bl][...], pages[2 * bl + 1][...]], axis=1)
        j = c * blocks_per_step + bl
        s = _dot(qbd, kt.astype(BF16))
        s_ref[j] = s
        gate = jnp.where(g_lane == j, jnp.sum(s, axis=1, keepdims=True) * (1.0 / MOBA_BLOCK), gate)
    g_ref[...] = gate

    @pl.when(last)
    def _():
        rank = jnp.zeros(gate.shape, F32)
        for kk in range(nblocks):
            gk = gate[:, kk:kk + 1]
            rank = rank + jnp.where((gk > gate) | ((gk == gate) & (kk < g_lane)), 1.0, 0.0)
        unsel = jnp.where(rank < MOBA_TOPK, 0.0, NEG)
        slope = _row_slopes(t)
        _, qq = _head_rows(t)
        within = slope * lax.broadcasted_iota(jnp.int32, (rows, MOBA_BLOCK), 1).astype(F32)
        kn = jnp.concatenate([kn_ref[...], jnp.zeros((LANES - t, D_ATTN), F32)], axis=0).astype(BF16)
        r_own = lax.broadcasted_iota(jnp.int32, (rows, LANES), 1)
        s_own = jnp.where(r_own <= qq, _nt_dot(qbd, kn) + within[:, :LANES], NEG)
        m_run = jnp.full((rows, MOBA_BLOCK), NEG, F32)
        for j in range(nblocks):
            sj = s_ref[j] + within + (slope * float(j * MOBA_BLOCK - past_len) + unsel[:, j:j + 1])
            s_ref[j] = sj
            m_run = jnp.maximum(m_run, sj)
        m = jnp.maximum(jnp.max(m_run, axis=1, keepdims=True), jnp.max(s_own, axis=1, keepdims=True))
        p_own = jnp.exp(s_own - m)
        po_ref[...] = p_own.astype(po_ref.dtype)
        l_run = jnp.zeros((rows, MOBA_BLOCK), F32)
        for j in range(nblocks):
            pj = jnp.exp(s_ref[j] - m)
            pp_ref[j] = pj.astype(pp_ref.dtype)
            l_run = l_run + pj
        l = jnp.sum(l_run, axis=1, keepdims=True) + jnp.sum(p_own, axis=1, keepdims=True)
        inv_ref[...] = jnp.broadcast_to(1.0 / l, inv_ref.shape)


def _svals_body(c, last, pp_ref, po_ref, inv_ref, vn_ref, pages, o_ref, acc_ref, *, t, host=None):
    @pl.when(c == 0)
    def _():
        vn = jnp.concatenate([vn_ref[...], jnp.zeros((LANES - t, D_ATTN), F32)], axis=0).astype(BF16)
        acc_ref[...] = _dot(po_ref[...], vn)

    if host is not None:
        host()
    acc = acc_ref[...]
    for bl in range(len(pages) // 2):
        vt = jnp.concatenate([pages[2 * bl][...], pages[2 * bl + 1][...]], axis=1).astype(BF16)
        acc = acc + _nt_dot(pp_ref[bl], vt)
    acc_ref[...] = acc

    @pl.when(last)
    def _():
        scaled = acc * inv_ref[:, 0:1]
        lane_head = lax.broadcasted_iota(jnp.int32, (t, D_ATTN), 1) // HEAD_DIM
        out = jnp.zeros((t, D_ATTN), F32)
        for hh in range(N_HEADS):
            out = jnp.where(lane_head == hh, scaled[hh * t:(hh + 1) * t, :], out)
        o_ref[...] = out


def _merge_kernel(o_ref, sa_ref, u_ref, up_ref, sb_ref, ga_ref, gb_ref, x_ref,
                  wp_ref, ps_ref, wba_ref, wbp_ref, wo_ref, gp_ref, y_ref, ubuf_ref,
                  *, ns, tm, t0, zero_first_prev):
    i = pl.program_id(1)
    n = ns * tm
    prev = up_ref[...]
    if zero_first_prev:
        prev = jnp.where(i == 0, 0.0, prev)
    ubuf_ref[:, 0:PREV_ROWS, :] = prev
    ubuf_ref[:, PREV_ROWS:PREV_ROWS + tm, :] = u_ref[...]
    pos = t0 + i * tm + lax.broadcasted_iota(jnp.int32, (ns, tm, POOL_GROUP), 1)
    mixed = []
    for gidx, w in enumerate(POOL_WINDOWS):
        gl = slice(gidx * POOL_GROUP, (gidx + 1) * POOL_GROUP)
        cur = ubuf_ref[:, PREV_ROWS:PREV_ROWS + tm, gl]
        tot = cur
        for s in range(1, w):
            tot = tot + ubuf_ref[:, PREV_ROWS - s:PREV_ROWS - s + tm, gl]
        cnt = jnp.minimum(pos + 1, w).astype(F32)
        pooled = (tot / cnt - cur).reshape(n, POOL_GROUP)
        mixed.append(_dot(pooled.astype(BF16), wp_ref[gidx]))
    mixed = jnp.concatenate(mixed, axis=1) * ps_ref[...]

    def flat(ref):
        return ref[...].reshape(n, ref.shape[-1]).astype(F32)

    o_b = (mixed * flat(sb_ref)).astype(BF16)
    o_a = (flat(o_ref) * flat(sa_ref)).astype(BF16)
    m_a = _dot(o_a, wba_ref[...])
    m_b = _dot(o_b, wbp_ref[...])
    merged = flat(ga_ref) * m_a + flat(gb_ref) * m_b
    z = _dot(merged.astype(BF16), wo_ref[...])
    ms = jnp.mean(z * z, axis=-1, keepdims=True)
    y = flat(x_ref) + z * lax.rsqrt(ms + RMS_EPS) * gp_ref[...]
    y_ref[...] = y.reshape(y_ref.shape)


def _merge_vals_kernel(pt_ref, *refs, geo, merge_kw):
    n_in = 14
    merge_in, (pp_ref, po_ref, inv_ref, vn_ref, cache_ref) = refs[:n_in], refs[n_in:n_in + 5]
    y_ref, os_ref, ubuf_ref, acc_ref, buf_ref, sem_ref = refs[n_in + 5:]
    step = pl.program_id(0) * pl.num_programs(1) + pl.program_id(1)
    _start_pages(pt_ref, cache_ref, buf_ref, sem_ref, step, pl.num_programs(0) * pl.num_programs(1), geo.pps)
    pages = _wait_pages(pt_ref, cache_ref, buf_ref, sem_ref, step, geo.pps)
    c = step % geo.spp
    _svals_body(c, c == geo.spp - 1, pp_ref, po_ref, inv_ref, vn_ref, pages, os_ref, acc_ref, t=geo.t,
                host=lambda: _merge_kernel(*merge_in, y_ref, ubuf_ref, **merge_kw))


def _merge(o, sa, u, u_prev, sb, ga, gb, x, ws, *, ns, tm, t0, zero_first_prev, vals=None):
    nseq, seq_len, d_model = x.shape
    assert nseq % ns == 0 and seq_len % tm == 0 and tm % SUBLANES == 0
    assert ns == 1 or (tm == SUBLANES and all(a.dtype == F32 for a in (o, sa, sb, ga, gb)))
    tile = lambda width: pl.BlockSpec((ns, tm, width), lambda b, i, *_: (b, i, 0))
    full = lambda a: pl.BlockSpec(a.shape, lambda b, i, *_: (0,) * a.ndim, pipeline_mode=pl.Buffered(1))
    per = tm // PREV_ROWS if tm >= PREV_ROWS else 0
    prev_spec = pl.BlockSpec((ns, PREV_ROWS, D_POOL),
                             lambda b, i, *_: (b, jnp.maximum(i * per - 1, 0), 0))
    merge_kw = dict(ns=ns, tm=tm, t0=t0, zero_first_prev=zero_first_prev)
    in_specs = [tile(D_ATTN), tile(D_ATTN), tile(D_POOL), prev_spec, tile(D_POOL),
                tile(d_model), tile(d_model), tile(d_model)] + [full(w) for w in ws]
    ubuf = pltpu.VMEM((ns, PREV_ROWS + tm, D_POOL), F32)
    grid = (nseq // ns, seq_len // tm)
    if vals is None:
        return pl.pallas_call(
            functools.partial(_merge_kernel, **merge_kw), out_shape=jax.ShapeDtypeStruct(x.shape, F32),
            grid=grid, in_specs=in_specs, out_specs=tile(d_model), scratch_shapes=[ubuf],
            compiler_params=_params("parallel", "arbitrary"),
            name="merge_prompt" if ns == 1 else "merge_sample",
        )(o, sa, u, u_prev, sb, ga, gb, x, *ws)

    pp, po, inv, vn, cache_v, page_table, t = vals
    geo = _stream_geometry(page_table, cache_v, t, grid[0] * grid[1])
    step = lambda b, i: b * grid[1] + i
    seq_of = lambda b, i: step(b, i) // geo.spp
    new_spec = pl.BlockSpec((t, D_ATTN), lambda b, i, pt_ref: (seq_of(b, i), 0))
    per_seq = pl.BlockSpec((None, geo.rows, LANES), lambda b, i, pt_ref: (seq_of(b, i), 0, 0))
    return pl.pallas_call(
        functools.partial(_merge_vals_kernel, geo=geo, merge_kw=merge_kw),
        out_shape=[jax.ShapeDtypeStruct(x.shape, F32), jax.ShapeDtypeStruct((geo.db * t, D_ATTN), F32)],
        grid_spec=pltpu.PrefetchScalarGridSpec(
            num_scalar_prefetch=1, grid=grid,
            in_specs=in_specs + [pl.BlockSpec((None, geo.pps // 2, geo.rows, MOBA_BLOCK),
                                              lambda b, i, pt_ref: (seq_of(b, i), step(b, i) % geo.spp, 0, 0)),
                                 per_seq, per_seq, new_spec, pl.BlockSpec(memory_space=pl.ANY)],
            out_specs=[tile(d_model), new_spec],
            scratch_shapes=[ubuf, pltpu.VMEM((geo.rows, D_ATTN), F32)] + _page_buffers(geo)),
        compiler_params=_params("arbitrary", "arbitrary"), name="merge_prompt_values",
    )(page_table.reshape(-1).astype(jnp.int32), o, sa, u, u_prev, sb, ga, gb, x, *ws,
      pp, po, inv, vn, cache_v)


def _layer(yp, ys, cache_k, cache_v, state, page_table, g_pre, w_in, w_pool, pool_scale,
           w_br_attn, w_br_pool, w_out, g_post):
    batch, seq_len, d_model = yp.shape
    db, t, _ = ys.shape
    past_len = page_table.shape[1] * cache_k.shape[2]

    bounds = [0]
    for width in (D_ATTN, D_ATTN, D_ATTN, D_ATTN, D_POOL, D_POOL, d_model, d_model):
        bounds.append(bounds[-1] + width)
    wq, wk, wv, wza, wu, wzb, wga, wgb = [w_in[:, a:b] for a, b in zip(bounds[:-1], bounds[1:])]
    proj_ws = [w.astype(BF16) for w in (wq * HEAD_DIM ** -0.5, wk, wv, wza, wu, wzb, wga, wgb)]
    g_pre2 = g_pre.reshape(1, d_model)
    merge_ws = [w_pool.astype(BF16), pool_scale.reshape(1, D_POOL), w_br_attn.astype(BF16),
                w_br_pool.astype(BF16), w_out.astype(BF16), g_post.reshape(1, d_model)]

    ks, vs, us, qs, sas, sbs, gas, gbs = _project(
        ys.reshape(db * t, d_model), g_pre2, proj_ws, prompt=False, seq_len=t)

    (kt, vt, up, qt, ka, va, sa, sb, ga, gb, km), (pp, po, inv) = _project(
        yp.reshape(batch * seq_len, d_model), g_pre2, proj_ws, prompt=True, seq_len=seq_len,
        keys=(qs, ks, cache_k, page_table, t))
    op = _moba_prompt(qt, ka, va, km, batch, seq_len)
    r3 = lambda a: a.reshape(batch, seq_len, a.shape[-1])
    up3 = r3(up)
    y_prompt, osamp = _merge(op, r3(sa), up3, up3, r3(sb), r3(ga), r3(gb), yp, merge_ws,
                             ns=1, tm=MOBA_BLOCK, t0=0, zero_first_prev=True,
                             vals=(pp, po, inv, vs, cache_v, page_table, t))
    seq_major = lambda a: jnp.transpose(a.reshape(batch, N_HEADS, HEAD_DIM, seq_len), (0, 3, 1, 2))

    s3 = lambda a: a.reshape(db, t, a.shape[-1])
    us3 = s3(us)
    state16 = jnp.concatenate([jnp.zeros((db, PREV_ROWS - POOL_STATE, D_POOL), F32), state], axis=1)
    y_sample = _merge(s3(osamp), s3(sas), us3, state16, s3(sbs), s3(gas), s3(gbs), ys, merge_ws,
                      ns=16, tm=t, t0=past_len, zero_first_prev=False)

    pool_prompt = up3[:, seq_len - POOL_STATE:]
    pool_sample = jnp.concatenate([state, us3], axis=1)[:, -POOL_STATE:]
    return (y_prompt, y_sample, seq_major(kt), seq_major(vt), pool_prompt,
            ks.reshape(db, t, N_HEADS, HEAD_DIM), vs.reshape(db, t, N_HEADS, HEAD_DIM), pool_sample)


def kernel(x_prompt, x_sample, cache_k, cache_v, state_pool, page_table, g_pre, w_in, w_pool,
           pool_scale, w_br_attn, w_br_pool, w_out, g_post):
    depth = w_in.shape[0]
    n_phys, page = cache_k.shape[1], cache_k.shape[2]
    pages_t = lambda c: jnp.transpose(c, (0, 2, 3, 1)).reshape(n_phys, D_ATTN, page)
    yp, ys = x_prompt, x_sample
    per_layer = []
    for l in range(depth):
        outs = _layer(yp, ys, pages_t(cache_k[l]), pages_t(cache_v[l]),
                      state_pool[l], page_table, g_pre[l], w_in[l], w_pool[l], pool_scale[l],
                      w_br_attn[l], w_br_pool[l], w_out[l], g_post[l])
        yp, ys = outs[0], outs[1]
        per_layer.append(outs[2:])
    stacked = [jnp.stack([layer[i] for layer in per_layer]) for i in range(6)]
    return (yp, ys, *stacked)
```
